```python
import math
import jax
import jax.numpy as jnp
from jax import lax
import numpy as np

D_MODEL = 2048
BATCH = 1
SEQ = 8192
DEPTH = 2

GRID_W = 64
CTX_LEN = 256
CHUNK = 128
Q_BLOCK = 128
ROPE_BASE = 10000.0
EPS = 1e-6
N_MOD = 6

D_SGU = D_MODEL // 4
D_RET = D_MODEL // 4
D_DIFF = D_MODEL // 2
D_MIX = D_SGU + D_RET + D_DIFF
SGU_DIM = 128
SGU_GROUPS = D_SGU // SGU_DIM
RET_DK = 128
RET_DV = 128
RET_HEADS = D_RET // RET_DV
DIFF_DV = 128
DIFF_DQK = DIFF_DV // 2
DIFF_HEADS = D_DIFF // DIFF_DV
D_FF = ((8 * D_MODEL // 3 + 255) // 256) * 256

IN_SIZES = (D_SGU, D_SGU, RET_HEADS * RET_DK, RET_HEADS * RET_DK, D_RET, D_RET, D_DIFF, D_DIFF, D_DIFF)
D_IN = 2 * D_SGU + 2 * RET_HEADS * RET_DK + 2 * D_RET + 3 * D_DIFF
P_SGU_U, P_SGU_V, P_RET_Q, P_RET_K, P_RET_V, P_RET_G, P_DIFF_Q, P_DIFF_K, P_DIFF_V = range(9)
CTX_KV_PARTS = (P_RET_K, P_RET_V, P_DIFF_K, P_DIFF_V)

kernel_name = 'hybrid_sgu_retention_diffattn_dit'


def rms_norm(x, g=None):
    xf = x.astype(jnp.float32)
    y = xf * lax.rsqrt(jnp.mean(xf * xf, axis=-1, keepdims=True) + EPS)
    if g is not None:
        y = y * g.astype(jnp.float32)
    return y.astype(x.dtype)


def modulate(x, g, shift, scale):
    return rms_norm(x, g) * (1 + scale) + shift


def split_in(z):
    offs = []
    acc = 0
    for s in IN_SIZES[:-1]:
        acc += s
        offs.append(acc)
    return jnp.split(z, offs, axis=-1)


def heads(z, n_heads):
    b, l, _ = z.shape
    return z.reshape(b, l, n_heads, -1).transpose(0, 2, 1, 3)


def merge_heads(z):
    b, h, l, d = z.shape
    return z.transpose(0, 2, 1, 3).reshape(b, l, h * d)


def flip(z):
    return z[:, :, ::-1]


def rope_1d(x, pos):
    half = x.shape[-1] // 2
    freqs = ROPE_BASE ** (-jnp.arange(half, dtype=jnp.float32) / half)
    ang = pos.astype(jnp.float32)[:, None] * freqs[None, :]
    cos, sin = jnp.cos(ang), jnp.sin(ang)
    xf = x.astype(jnp.float32)
    x1, x2 = xf[..., :half], xf[..., half:]
    return jnp.concatenate([x1 * cos - x2 * sin, x1 * sin + x2 * cos], axis=-1).astype(x.dtype)


def axial_rope(x, rows, cols):
    h = x.shape[-1] // 2
    return jnp.concatenate([rope_1d(x[..., :h], rows), rope_1d(x[..., h:], cols)], axis=-1)


def chunk_sgu(u, v, w_s, b_s):
    b, l, _ = v.shape
    n = l // CHUNK
    vg = rms_norm(v.reshape(b, n, CHUNK, SGU_GROUPS, SGU_DIM))
    mixed = jnp.einsum('gij,bnjgc->bnigc', w_s, vg) + b_s.T[None, None, :, :, None]
    return u * mixed.reshape(b, l, D_SGU)


def retention_chunkwise(q, k, v, log_gamma, s0):
    b, h, l, _ = q.shape
    n = l // CHUNK

    def to_chunks(z):
        return jnp.moveaxis(z.reshape(b, h, n, CHUNK, z.shape[-1]), 2, 0)

    idx = jnp.arange(CHUNK, dtype=jnp.float32)
    lg = log_gamma[:, None]
    rel = idx[:, None] - idx[None, :]
    intra = jnp.where(rel >= 0, jnp.exp(jnp.maximum(rel, 0.0)[None] * log_gamma[:, None, None]), 0.0).astype(q.dtype)
    q_decay = jnp.exp((idx + 1.0)[None, :] * lg).astype(q.dtype)[None, :, :, None]
    k_decay = jnp.exp((CHUNK - 1.0 - idx)[None, :] * lg).astype(q.dtype)[None, :, :, None]
    c_decay = jnp.exp(CHUNK * log_gamma).astype(q.dtype)[None, :, None, None]

    def step(state, chunk):
        qc, kc, vc = chunk
        scores = jnp.einsum('bhid,bhjd->bhij', qc, kc) * intra[None]
        out = jnp.einsum('bhij,bhjv->bhiv', scores, vc) + jnp.einsum('bhid,bhdv->bhiv', qc * q_decay, state)
        state = state * c_decay + jnp.einsum('bhjd,bhjv->bhdv', kc * k_decay, vc)
        return state, out

    _, outs = lax.scan(step, s0.astype(q.dtype), (to_chunks(q), to_chunks(k), to_chunks(v)))
    return jnp.moveaxis(outs, 0, 2).reshape(b, h, l, v.shape[-1])


def retention_final_state(k, v, log_gamma):
    l = k.shape[2]
    w = jnp.exp((l - 1.0 - jnp.arange(l, dtype=jnp.float32))[None, :] * log_gamma[:, None]).astype(k.dtype)
    return jnp.einsum('bhjd,hj,bhjv->bhdv', k, w, v)


def retention_mixer(q, k, v, g, qc, kc, vc, gc, lg_f, lg_b, rows, cols):
    scale = RET_DK ** -0.5
    q = axial_rope(heads(q, RET_HEADS), rows, cols)
    k = axial_rope(heads(k, RET_HEADS), rows, cols) * scale
    v = heads(v, RET_HEADS)
    kc = heads(kc, RET_HEADS) * scale
    vc = heads(vc, RET_HEADS)
    s_f = retention_final_state(kc, vc, lg_f)
    s_b = retention_final_state(flip(kc), flip(vc), lg_b)
    o = retention_chunkwise(q, k, v, lg_f, s_f) + flip(retention_chunkwise(flip(q), flip(k), flip(v), lg_b, s_b))
    y = merge_heads(rms_norm(o)) * jax.nn.silu(g)
    if qc is None:
        return y, None
    qc = heads(qc, RET_HEADS)
    zero = jnp.zeros_like(s_f)
    oc = retention_chunkwise(qc, kc, vc, lg_f, zero) + flip(retention_chunkwise(flip(qc), flip(kc), flip(vc), lg_b, zero))
    yc = merge_heads(rms_norm(oc)) * jax.nn.silu(gc)
    return y, yc


def diff_attend(q1, q2, k1, k2, v, lam):
    scale = DIFF_DQK ** -0.5
    s1 = jnp.einsum('bhqd,bhkd->bhqk', q1, k1).astype(jnp.float32) * scale
    s2 = jnp.einsum('bhqd,bhkd->bhqk', q2, k2).astype(jnp.float32) * scale
    p = jax.nn.softmax(s1, axis=-1) - lam * jax.nn.softmax(s2, axis=-1)
    return jnp.einsum('bhqk,bhkv->bhqv', p.astype(v.dtype), v)


def diff_attention_mixer(q, k, v, qc, kc, vc, lam, lam_init, subln_g, rows, cols):
    b, l, _ = q.shape

    def maps(z, rope):
        zh = heads(z, DIFF_HEADS)
        z1, z2 = zh[..., :DIFF_DQK], zh[..., DIFF_DQK:]
        if rope:
            z1, z2 = axial_rope(z1, rows, cols), axial_rope(z2, rows, cols)
        return z1, z2

    q1, q2 = maps(q, True)
    k1, k2 = maps(k, True)
    kc1, kc2 = maps(kc, False)
    vch = heads(vc, DIFF_HEADS)
    key1 = jnp.concatenate([k1, kc1], axis=2)
    key2 = jnp.concatenate([k2, kc2], axis=2)
    val = jnp.concatenate([heads(v, DIFF_HEADS), vch], axis=2)
    nb = l // Q_BLOCK

    def blocks(z):
        return jnp.moveaxis(z.reshape(b, DIFF_HEADS, nb, Q_BLOCK, z.shape[-1]), 2, 0)

    o = lax.map(lambda qs: diff_attend(qs[0], qs[1], key1, key2, val, lam), (blocks(q1), blocks(q2)))
    o = jnp.moveaxis(o, 0, 2).reshape(b, DIFF_HEADS, l, DIFF_DV)
    y = merge_heads(rms_norm(o, subln_g) * (1.0 - lam_init))
    if qc is None:
        return y, None
    qc1, qc2 = maps(qc, False)
    oc = diff_attend(qc1, qc2, kc1, kc2, vch, lam)
    yc = merge_heads(rms_norm(oc, subln_g) * (1.0 - lam_init))
    return y, yc


def swiglu(h, wg, wu, wd):
    return (jax.nn.silu(h @ wg) * (h @ wu)) @ wd


def setup_inputs(seed: int = 0) -> dict:
    key = jax.random.key(seed)
    ks = jax.random.split(key, 24)
    f32 = jnp.float32

    def nrm(k, shape, scale):
        return jax.random.normal(k, shape, f32) * scale

    gamma = 1.0 - jnp.exp(jnp.linspace(math.log(1.0 / 32), math.log(1.0 / 512), RET_HEADS))
    base_logit = jnp.log(gamma) - jnp.log1p(-gamma)
    return {
        'x': nrm(ks[0], (BATCH, SEQ, D_MODEL), 1.0),
        'c': nrm(ks[1], (BATCH, D_MODEL), 1.0),
        'ctx': nrm(ks[2], (BATCH, CTX_LEN, D_MODEL), 1.0),
        'c_ctx': nrm(ks[3], (D_MODEL,), 1.0),
        'w_ada': nrm(ks[4], (DEPTH, D_MODEL, N_MOD * D_MODEL), 0.5 * D_MODEL ** -0.5),
        'b_ada': nrm(ks[5], (DEPTH, N_MOD * D_MODEL), 0.02),
        'norm1_g': 1.0 + nrm(ks[6], (DEPTH, D_MODEL), 0.02),
        'w_in': nrm(ks[7], (DEPTH, D_MODEL, D_IN), D_MODEL ** -0.5),
        'sgu_w': nrm(ks[8], (DEPTH, SGU_GROUPS, CHUNK, CHUNK), CHUNK ** -0.5),
        'sgu_b': 1.0 + nrm(ks[9], (DEPTH, SGU_GROUPS, CHUNK), 0.02),
        'ret_decay_fwd': base_logit[None, :] + nrm(ks[10], (DEPTH, RET_HEADS), 0.05),
        'ret_decay_bwd': base_logit[None, :] + nrm(ks[11], (DEPTH, RET_HEADS), 0.05),
        'diff_lambda_q1': nrm(ks[12], (DEPTH, DIFF_DQK), 0.1),
        'diff_lambda_k1': nrm(ks[13], (DEPTH, DIFF_DQK), 0.1),
        'diff_lambda_q2': nrm(ks[14], (DEPTH, DIFF_DQK), 0.1),
        'diff_lambda_k2': nrm(ks[15], (DEPTH, DIFF_DQK), 0.1),
        'diff_subln_g': 1.0 + nrm(ks[16], (DEPTH, DIFF_DV), 0.02),
        'w_out': nrm(ks[17], (DEPTH, D_MIX, D_MODEL), D_MIX ** -0.5),
        'norm2_g': 1.0 + nrm(ks[18], (DEPTH, D_MODEL), 0.02),
        'w_gate': nrm(ks[19], (DEPTH, D_MODEL, D_FF), D_MODEL ** -0.5),
        'w_up': nrm(ks[20], (DEPTH, D_MODEL, D_FF), D_MODEL ** -0.5),
        'w_down': nrm(ks[21], (DEPTH, D_FF, D_MODEL), D_FF ** -0.5),
        'final_g': 1.0 + nrm(ks[22], (D_MODEL,), 0.02),
    }


def reference(x, c, ctx, c_ctx, w_ada, b_ada, norm1_g, w_in, sgu_w, sgu_b, ret_decay_fwd, ret_decay_bwd,
              diff_lambda_q1, diff_lambda_k1, diff_lambda_q2, diff_lambda_k2, diff_subln_g, w_out, norm2_g,
              w_gate, w_up, w_down, final_g):
    seq_len = x.shape[1]
    n_rows = seq_len // GRID_W
    rows = jnp.repeat(jnp.arange(n_rows, dtype=jnp.int32), GRID_W)
    cols = jnp.tile(jnp.arange(GRID_W, dtype=jnp.int32), n_rows)
    xc = ctx
    c_act = jax.nn.silu(c)[:, None, :]
    cc_act = jax.nn.silu(c_ctx)[None, None, :]
    n_parts = len(IN_SIZES)
    for i in range(DEPTH):
        need_ctx = i < DEPTH - 1
        sh1, sc1, g1, sh2, sc2, g2 = jnp.split(c_act @ w_ada[i] + b_ada[i], N_MOD, axis=-1)
        csh1, csc1, cg1, csh2, csc2, cg2 = jnp.split(cc_act @ w_ada[i] + b_ada[i], N_MOD, axis=-1)

        h = modulate(x, norm1_g[i], sh1, sc1)
        p = split_in(h @ w_in[i])
        hc = modulate(xc, norm1_g[i], csh1, csc1)
        w_parts = split_in(w_in[i])
        needed = tuple(range(n_parts)) if need_ctx else CTX_KV_PARTS
        pc = [hc @ w_parts[j] if j in needed else None for j in range(n_parts)]

        y_a = chunk_sgu(jax.nn.gelu(p[P_SGU_U]), jax.nn.gelu(p[P_SGU_V]), sgu_w[i], sgu_b[i])
        lg_f = jax.nn.log_sigmoid(ret_decay_fwd[i].astype(jnp.float32))
        lg_b = jax.nn.log_sigmoid(ret_decay_bwd[i].astype(jnp.float32))
        y_b, yc_b = retention_mixer(p[P_RET_Q], p[P_RET_K], p[P_RET_V], p[P_RET_G],
                                    pc[P_RET_Q], pc[P_RET_K], pc[P_RET_V], pc[P_RET_G],
                                    lg_f, lg_b, rows, cols)
        lam_init = 0.8 - 0.6 * math.exp(-0.3 * i)
        lam = (jnp.exp(jnp.sum(diff_lambda_q1[i].astype(jnp.float32) * diff_lambda_k1[i].astype(jnp.float32)))
               - jnp.exp(jnp.sum(diff_lambda_q2[i].astype(jnp.float32) * diff_lambda_k2[i].astype(jnp.float32)))
               + lam_init)
        y_c, yc_c = diff_attention_mixer(p[P_DIFF_Q], p[P_DIFF_K], p[P_DIFF_V],
                                         pc[P_DIFF_Q], pc[P_DIFF_K], pc[P_DIFF_V],
                                         lam, lam_init, diff_subln_g[i], rows, cols)

        x = x + g1 * (jnp.concatenate([y_a, y_b, y_c], axis=-1) @ w_out[i])
        x = x + g2 * swiglu(modulate(x, norm2_g[i], sh2, sc2), w_gate[i], w_up[i], w_down[i])

        if need_ctx:
            yc_a = chunk_sgu(jax.nn.gelu(pc[P_SGU_U]), jax.nn.gelu(pc[P_SGU_V]), sgu_w[i], sgu_b[i])
            xc = xc + cg1 * (jnp.concatenate([yc_a, yc_b, yc_c], axis=-1) @ w_out[i])
            xc = xc + cg2 * swiglu(modulate(xc, norm2_g[i], csh2, csc2), w_gate[i], w_up[i], w_down[i])
    return rms_norm(x, final_g)
```

```python
import functools
import math

import jax
import jax.numpy as jnp
from jax import lax
from jax.experimental import pallas as pl
from jax.experimental.pallas import tpu as pltpu

GRID_W = 64
ROPE_BASE = 10000.0
EPS = 1e-6
N_MOD = 6
HEAD = 128
SGU_CHUNK = 128
RET_BLOCK = 256
IN_TILES = 12
VMEM_LIMIT_BYTES = 62 * 1024 * 1024

F32 = jnp.float32
BF16 = jnp.bfloat16


def _cparams(*sem):
    return pltpu.CompilerParams(dimension_semantics=sem, vmem_limit_bytes=VMEM_LIMIT_BYTES)


def _tile(n, want):
    if n <= want:
        return n
    t = want
    while n % t:
        t -= 8
    return t


def _ada_kernel(a_ref, w_ref, b_ref, o_ref, *, rows):
    d, tn = w_ref.shape[1], w_ref.shape[2]

    def body(r, acc):
        r0 = pl.multiple_of(r * rows, rows)
        a = a_ref[pl.ds(r0, rows), :]
        a = a * jax.nn.sigmoid(a)
        w = w_ref[0, pl.ds(r0, rows), :]
        acc0, acc1 = acc
        acc0 = acc0 + (w * a[:, 0:1]).reshape(rows // 8, 8, tn).sum(axis=0)
        acc1 = acc1 + (w * a[:, 1:2]).reshape(rows // 8, 8, tn).sum(axis=0)
        return acc0, acc1

    z = jnp.zeros((8, tn), F32)
    acc0, acc1 = lax.fori_loop(0, d // rows, body, (z, z))
    out = jnp.concatenate([acc0.sum(axis=0, keepdims=True), acc1.sum(axis=0, keepdims=True)], axis=0)
    o_ref[0] = out + b_ref[0]


def _ada(c2t, w_ada, b_ada):
    depth, d, n = w_ada.shape
    tn = _tile(n, 1024)
    return pl.pallas_call(
        functools.partial(_ada_kernel, rows=64),
        out_shape=jax.ShapeDtypeStruct((depth, 2, n), F32),
        grid=(depth, n // tn),
        in_specs=[
            pl.BlockSpec((d, 2), lambda l, j: (0, 0)),
            pl.BlockSpec((1, d, tn), lambda l, j: (l, 0, j)),
            pl.BlockSpec((1, 1, tn), lambda l, j: (l, 0, j)),
        ],
        out_specs=pl.BlockSpec((1, 2, tn), lambda l, j: (l, 0, j)),
        compiler_params=_cparams("parallel", "parallel"),
        name="ada",
    )(c2t, w_ada, b_ada.reshape(depth, 1, n))


def _rope(x, c, s1, s2, shift):
    return x * c + pltpu.roll(x, HEAD - shift, 1) * s1 + pltpu.roll(x, shift, 1) * s2


def _group_rms(x):
    return x * lax.rsqrt(jnp.mean(x * x, axis=-1, keepdims=True) + EPS)


def _modulated_norm(x, g, sh, sc):
    y = x * lax.rsqrt(jnp.mean(x * x, axis=-1, keepdims=True) + EPS) * g
    return y * (1.0 + sc) + sh


def _inproj_kernel(x_ref, sh_ref, sc_ref, g_ref, w_ref, tab_ref, *rest, aliased):
    if aliased:
        _, o_ref, h_ref, acc_ref = rest
    else:
        o_ref, h_ref, acc_ref = rest
    j = pl.program_id(1)
    tn = w_ref.shape[1]
    nh = tn // HEAD

    @pl.when(j == 0)
    def _():
        h_ref[...] = _modulated_norm(x_ref[...], g_ref[...], sh_ref[...], sc_ref[...]).astype(BF16)

    acc_ref[...] = jnp.dot(h_ref[...], w_ref[...], preferred_element_type=F32)

    def store_heads(fn):
        for hh in range(nh):
            sl = slice(hh * HEAD, (hh + 1) * HEAD)
            o_ref[:, sl] = fn(acc_ref[:, sl]).astype(BF16)

    def rope_ret(scale):
        c, s1, s2 = tab_ref[0], tab_ref[1], tab_ref[2]
        return lambda a: _rope(a, c, s1, s2, HEAD // 4) * scale

    def rope_diff(scale):
        c, s1, s2 = tab_ref[3], tab_ref[4], tab_ref[5]
        return lambda a: _rope(a, c, s1, s2, HEAD // 8) * scale

    @pl.when(j == 0)
    def _():
        store_heads(jax.nn.gelu)

    @pl.when(j == 1)
    def _():
        store_heads(lambda a: _group_rms(jax.nn.gelu(a)))

    @pl.when(j == 2)
    def _():
        store_heads(rope_ret(1.0))

    @pl.when(j == 3)
    def _():
        store_heads(rope_ret(HEAD ** -0.5))

    @pl.when((j == 4) | (j >= 10))
    def _():
        store_heads(lambda a: a)

    @pl.when(j == 5)
    def _():
        store_heads(lambda a: a * jax.nn.sigmoid(a))

    @pl.when((j == 6) | (j == 7))
    def _():
        store_heads(rope_diff((HEAD // 2) ** -0.5))

    @pl.when((j == 8) | (j == 9))
    def _():
        store_heads(rope_diff(1.0))


def _inproj(x, mods, li, row, g, w, tabs, total_rows, row_off, p_prev=None):
    m, d = x.shape
    n = w.shape[1]
    tn = n // IN_TILES
    tm = _tile(m, 1024)
    assert row_off % tm == 0
    ob = row_off // tm
    aliased = p_prev is not None
    in_specs = [
        pl.BlockSpec((tm, d), lambda i, j: (i, 0)),
        pl.BlockSpec((None, None, None, 1, d), lambda i, j: (li, row, 0, 0, 0)),
        pl.BlockSpec((None, None, None, 1, d), lambda i, j: (li, row, 1, 0, 0)),
        pl.BlockSpec((1, d), lambda i, j: (0, 0)),
        pl.BlockSpec((d, tn), lambda i, j: (0, j)),
        pl.BlockSpec((6, tm, HEAD), lambda i, j: (0, i, 0)),
    ]
    args = [x, mods, mods, g, w, tabs]
    if aliased:
        in_specs.append(pl.BlockSpec(memory_space=pl.ANY))
        args.append(p_prev)
    return pl.pallas_call(
        functools.partial(_inproj_kernel, aliased=aliased),
        out_shape=jax.ShapeDtypeStruct((total_rows, n), BF16),
        grid=(m // tm, IN_TILES),
        in_specs=in_specs,
        out_specs=pl.BlockSpec((tm, tn), lambda i, j: (ob + i, j)),
        scratch_shapes=[pltpu.VMEM((tm, d), BF16), pltpu.VMEM((tm, tn), F32)],
        input_output_aliases={6: 0} if aliased else {},
        compiler_params=_cparams("parallel", "arbitrary"),
        name="inproj",
    )(*args)


def _log_sigmoid(z):
    return jnp.minimum(z, 0.0) - jnp.log1p(jnp.exp(-jnp.abs(z)))


def _ret_kernel(dec_ref, qf_ref, kf_ref, vf_ref, qb_ref, kb_ref, vb_ref, of_ref, ob_ref,
                state_ref, mask_ref, qd_ref, kd_ref, cd_ref):
    s = pl.program_id(0)
    rc = qf_ref.shape[0]
    nh = qf_ref.shape[1] // HEAD

    @pl.when(s == 0)
    def _():
        state_ref[...] = jnp.zeros_like(state_ref)
        ii = lax.broadcasted_iota(jnp.int32, (rc, rc), 0).astype(F32)
        jj = lax.broadcasted_iota(jnp.int32, (rc, rc), 1).astype(F32)
        col = lax.broadcasted_iota(jnp.int32, (rc, HEAD), 0).astype(F32)
        for dr in range(2):
            rel = (ii - jj) if dr == 0 else (jj - ii)
            for h in range(nh):
                logit = dec_ref[dr, h]
                lg = _log_sigmoid(jnp.full((rc, rc), logit, F32))
                mask_ref[dr, h] = jnp.where(rel >= 0, jnp.exp(jnp.maximum(rel, 0.0) * lg), 0.0)
                lgc = _log_sigmoid(jnp.full((rc, HEAD), logit, F32))
                q_pow = (col + 1.0) if dr == 0 else (rc - col)
                k_pow = (rc - 1.0 - col) if dr == 0 else col
                qd_ref[dr, h] = jnp.exp(q_pow * lgc)
                kd_ref[dr, h] = jnp.exp(k_pow * lgc)
                cd_ref[dr, h] = jnp.exp(rc * _log_sigmoid(jnp.full((HEAD, HEAD), logit, F32)))

    for dr, (q_ref, k_ref, v_ref, o_ref) in enumerate(
            ((qf_ref, kf_ref, vf_ref, of_ref), (qb_ref, kb_ref, vb_ref, ob_ref))):
        for h in range(nh):
            sl = slice(h * HEAD, (h + 1) * HEAD)
            q = q_ref[:, sl]
            k = k_ref[:, sl]
            v = v_ref[:, sl]
            state = state_ref[dr, h]
            scores = lax.dot_general(q, k, (((1,), (1,)), ((), ())), preferred_element_type=F32)
            scores = (scores * mask_ref[dr, h]).astype(BF16)
            qs = (q.astype(F32) * qd_ref[dr, h]).astype(BF16)
            out = jnp.dot(scores, v, preferred_element_type=F32)
            out = out + jnp.dot(qs, state.astype(BF16), preferred_element_type=F32)
            o_ref[:, sl] = out
            ks = (k.astype(F32) * kd_ref[dr, h]).astype(BF16)
            upd = lax.dot_general(ks, v, (((0,), (0,)), ((), ())), preferred_element_type=F32)
            state_ref[dr, h] = state * cd_ref[dr, h] + upd


def _retention(p, dec, n_lat, n_ctx, d_ret):
    t = p.shape[0]
    rc = RET_BLOCK
    assert n_lat % rc == 0 and n_ctx % rc == 0
    nb, nlb, ncb = t // rc, n_lat // rc, n_ctx // rc
    nh = d_ret // HEAD
    cq, ck, cv = 2, 3, 4

    def fmap(col):
        return lambda s: (jnp.where(s < ncb, nlb + s, s - ncb), col)

    def bmap(col):
        return lambda s: (nb - 1 - s, col)

    blk = lambda im: pl.BlockSpec((rc, d_ret), im)
    return pl.pallas_call(
        _ret_kernel,
        out_shape=[jax.ShapeDtypeStruct((t, d_ret), F32)] * 2,
        grid=(nb,),
        in_specs=[pl.BlockSpec(memory_space=pltpu.SMEM),
                  blk(fmap(cq)), blk(fmap(ck)), blk(fmap(cv)),
                  blk(bmap(cq)), blk(bmap(ck)), blk(bmap(cv))],
        out_specs=[blk(fmap(0)), blk(bmap(0))],
        scratch_shapes=[
            pltpu.VMEM((2, nh, HEAD, HEAD), F32),
            pltpu.VMEM((2, nh, rc, rc), F32),
            pltpu.VMEM((2, nh, rc, HEAD), F32),
            pltpu.VMEM((2, nh, rc, HEAD), F32),
            pltpu.VMEM((2, nh, HEAD, HEAD), F32),
        ],
        compiler_params=_cparams("arbitrary"),
        name="retention",
    )(dec, p, p, p, p, p, p)


def _mix_kernel(u_ref, v_ref, gate_ref, of_ref, ob_ref, ws_ref, bs_ref, ya_ref, yb_ref):
    tm = u_ref.shape[0]
    ng = u_ref.shape[1] // HEAD
    for n in range(tm // SGU_CHUNK):
        rs = slice(n * SGU_CHUNK, (n + 1) * SGU_CHUNK)
        for g in range(ng):
            cs = slice(g * HEAD, (g + 1) * HEAD)
            mixed = jnp.dot(ws_ref[g], v_ref[rs, cs], preferred_element_type=F32) + bs_ref[g]
            ya_ref[rs, cs] = (u_ref[rs, cs].astype(F32) * mixed).astype(BF16)
    for h in range(of_ref.shape[1] // HEAD):
        cs = slice(h * HEAD, (h + 1) * HEAD)
        o = of_ref[:, cs] + ob_ref[:, cs]
        yb_ref[:, cs] = (_group_rms(o) * gate_ref[:, cs].astype(F32)).astype(BF16)


def _mix(p, o_f, o_b, ws, bs, dq):
    t = p.shape[0]
    tm = _tile(t, 256)
    row = lambda col: pl.BlockSpec((tm, dq), lambda i: (i, col))
    full = lambda a: pl.BlockSpec(a.shape, lambda i: (0,) * a.ndim)
    return pl.pallas_call(
        _mix_kernel,
        out_shape=[jax.ShapeDtypeStruct((t, dq), BF16)] * 2,
        grid=(t // tm,),
        in_specs=[row(0), row(1), row(5), row(0), row(0), full(ws), full(bs)],
        out_specs=[row(0), row(0)],
        compiler_params=_cparams("parallel"),
        name="mix",
    )(p, p, p, o_f, o_b, ws, bs)


def _attn_kernel(q_ref, k_ref, v_ref, lam_ref, g_ref, o_ref, acc1_ref, acc2_ref, *, tk, lam_init):
    tq = q_ref.shape[0]
    nkv = k_ref.shape[0] // tk
    q = q_ref[...]
    lane = lax.broadcasted_iota(jnp.int32, q.shape, 1)
    zero = jnp.zeros_like(q)
    q1 = jnp.where(lane < HEAD // 2, q, zero)
    q2 = jnp.where(lane >= HEAD // 2, q, zero)
    acc1_ref[...] = jnp.zeros_like(acc1_ref)
    acc2_ref[...] = jnp.zeros_like(acc2_ref)

    def one_map(qz, kb, vb, m, l, acc_ref):
        s = lax.dot_general(qz, kb, (((1,), (1,)), ((), ())), preferred_element_type=F32)
        m_new = jnp.maximum(m, s.max(axis=1, keepdims=True))
        alpha = jnp.exp(m - m_new)
        pr = jnp.exp(s - m_new)
        l_new = alpha * l + pr.sum(axis=1, keepdims=True)
        acc_ref[...] = alpha * acc_ref[...] + jnp.dot(pr.astype(BF16), vb, preferred_element_type=F32)
        return m_new, l_new

    def body(j, carry):
        m1, l1, m2, l2 = carry
        r0 = pl.multiple_of(j * tk, tk)
        kb = k_ref[pl.ds(r0, tk), :]
        vb = v_ref[pl.ds(r0, tk), :]
        m1, l1 = one_map(q1, kb, vb, m1, l1, acc1_ref)
        m2, l2 = one_map(q2, kb, vb, m2, l2, acc2_ref)
        return m1, l1, m2, l2

    neg = jnp.full((tq, 1), -jnp.inf, F32)
    zer = jnp.zeros((tq, 1), F32)
    _, l1, _, l2 = lax.fori_loop(0, nkv, body, (neg, zer, neg, zer))

    lv = lam_ref[...]
    lam = (jnp.exp(jnp.sum(lv[0:1] * lv[1:2], axis=1, keepdims=True))
           - jnp.exp(jnp.sum(lv[2:3] * lv[3:4], axis=1, keepdims=True)) + lam_init)
    o = acc1_ref[...] / l1 - lam * (acc2_ref[...] / l2)
    y = o * lax.rsqrt(jnp.mean(o * o, axis=-1, keepdims=True) + EPS) * g_ref[...]
    o_ref[...] = (y * (1.0 - lam_init)).astype(BF16)


def _attention(p, lam_vecs, subln_g, q_row0, n_q, k_row0, n_k, d_model, lam_init):
    d_diff = d_model // 2
    nh = d_diff // HEAD
    quarter = (d_model // 4) // HEAD
    cq, ck, cv = 6 * quarter, 8 * quarter, 10 * quarter
    tq = _tile(n_q, 512)
    tk = _tile(n_k, 768)
    assert q_row0 % tq == 0 and k_row0 % n_k == 0
    qb, kb = q_row0 // tq, k_row0 // n_k
    return pl.pallas_call(
        functools.partial(_attn_kernel, tk=tk, lam_init=lam_init),
        out_shape=jax.ShapeDtypeStruct((n_q, d_diff), BF16),
        grid=(nh, n_q // tq),
        in_specs=[
            pl.BlockSpec((tq, HEAD), lambda h, i: (qb + i, cq + h)),
            pl.BlockSpec((n_k, HEAD), lambda h, i: (kb, ck + h)),
            pl.BlockSpec((n_k, HEAD), lambda h, i: (kb, cv + h)),
            pl.BlockSpec(lam_vecs.shape, lambda h, i: (0, 0)),
            pl.BlockSpec((1, HEAD), lambda h, i: (0, 0)),
        ],
        out_specs=pl.BlockSpec((tq, HEAD), lambda h, i: (i, h)),
        scratch_shapes=[pltpu.VMEM((tq, HEAD), F32), pltpu.VMEM((tq, HEAD), F32)],
        compiler_params=_cparams("parallel", "parallel"),
        name="diffattn",
    )(p, p, p, lam_vecs, subln_g)


def _outproj_kernel(ya_ref, yb_ref, yc_ref, wa_ref, wb_ref, wc_ref, x_ref, gate_ref, o_ref):
    acc = jnp.dot(ya_ref[...], wa_ref[...], preferred_element_type=F32)
    acc = acc + jnp.dot(yb_ref[...], wb_ref[...], preferred_element_type=F32)
    acc = acc + jnp.dot(yc_ref[...], wc_ref[...], preferred_element_type=F32)
    o_ref[...] = x_ref[...] + gate_ref[...] * acc


def _outproj(ya, yb, yc, y_row0, w, x, mods, li, row):
    m, d = x.shape
    dq = d // 4
    tm = _tile(m, 512)
    assert y_row0 % tm == 0
    yb0 = y_row0 // tm
    return pl.pallas_call(
        _outproj_kernel,
        out_shape=jax.ShapeDtypeStruct((m, d), F32),
        grid=(m // tm,),
        in_specs=[
            pl.BlockSpec((tm, dq), lambda i: (yb0 + i, 0)),
            pl.BlockSpec((tm, dq), lambda i: (yb0 + i, 0)),
            pl.BlockSpec((tm, 2 * dq), lambda i: (i, 0)),
            pl.BlockSpec((dq, d), lambda i: (0, 0)),
            pl.BlockSpec((dq, d), lambda i: (1, 0)),
            pl.BlockSpec((2 * dq, d), lambda i: (1, 0)),
            pl.BlockSpec((tm, d), lambda i: (i, 0)),
            pl.BlockSpec((None, None, None, 1, d), lambda i: (li, row, 2, 0, 0)),
        ],
        out_specs=pl.BlockSpec((tm, d), lambda i: (i, 0)),
        compiler_params=_cparams("parallel"),
        name="outproj",
    )(ya, yb, yc, w, w, w, x, mods)


def _ffn_kernel(x_ref, sh_ref, sc_ref, gate_ref, g_ref, wg_ref, wu_ref, wd_ref, fg_ref, o_ref, h_ref,
                *, final_norm):
    f = pl.program_id(1)

    @pl.when(f == 0)
    def _():
        h_ref[...] = _modulated_norm(x_ref[...], g_ref[...], sh_ref[...], sc_ref[...]).astype(BF16)
        o_ref[...] = jnp.zeros_like(o_ref)

    h = h_ref[...]
    a = jnp.dot(h, wg_ref[...], preferred_element_type=F32)
    b = jnp.dot(h, wu_ref[...], preferred_element_type=F32)
    act = (a * jax.nn.sigmoid(a) * b).astype(BF16)
    o_ref[...] += jnp.dot(act, wd_ref[...], preferred_element_type=F32)

    @pl.when(f == pl.num_programs(1) - 1)
    def _():
        y = x_ref[...] + gate_ref[...] * o_ref[...]
        if final_norm:
            y = y * lax.rsqrt(jnp.mean(y * y, axis=-1, keepdims=True) + EPS) * fg_ref[...]
        o_ref[...] = y


def _ffn(x, mods, li, row, g, wg, wu, wd, final_g, final_norm):
    m, d = x.shape
    ff = wg.shape[1]
    tm = _tile(m, 1024)
    tf = _tile(ff, 256)
    mod = lambda c: pl.BlockSpec((None, None, None, 1, d), lambda i, f: (li, row, c, 0, 0))
    return pl.pallas_call(
        functools.partial(_ffn_kernel, final_norm=final_norm),
        out_shape=jax.ShapeDtypeStruct((m, d), F32),
        grid=(m // tm, ff // tf),
        in_specs=[
            pl.BlockSpec((tm, d), lambda i, f: (i, 0)),
            mod(3), mod(4), mod(5),
            pl.BlockSpec((1, d), lambda i, f: (0, 0)),
            pl.BlockSpec((d, tf), lambda i, f: (0, f)),
            pl.BlockSpec((d, tf), lambda i, f: (0, f)),
            pl.BlockSpec((tf, d), lambda i, f: (f, 0)),
            pl.BlockSpec((1, d), lambda i, f: (0, 0)),
        ],
        out_specs=pl.BlockSpec((tm, d), lambda i, f: (i, 0)),
        scratch_shapes=[pltpu.VMEM((tm, d), BF16)],
        compiler_params=_cparams("parallel", "arbitrary"),
        name="ffn",
    )(x, mods, mods, mods, g, wg, wu, wd, final_g)


def _rope_tables(n_rows, identity):
    def one(width):
        half = width // 2
        lane = jnp.arange(HEAD)
        sub = lane // width
        idx = lane % width
        first = idx < half
        freqs = ROPE_BASE ** (-(idx % half).astype(F32) / half)
        t = jnp.arange(n_rows)
        pos = jnp.where((sub % 2) == 0, (t // GRID_W)[:, None], (t % GRID_W)[:, None]).astype(F32)
        ang = pos * freqs[None, :]
        cos, sin = jnp.cos(ang), jnp.sin(ang)
        if identity:
            cos, sin = jnp.ones_like(cos), jnp.zeros_like(sin)
        return [cos, jnp.where(first[None, :], -sin, 0.0), jnp.where(first[None, :], 0.0, sin)]
    return jnp.stack(one(HEAD // 2) + one(HEAD // 4)).astype(F32)


def kernel(x, c, ctx, c_ctx, w_ada, b_ada, norm1_g, w_in, sgu_w, sgu_b, ret_decay_fwd, ret_decay_bwd,
           diff_lambda_q1, diff_lambda_k1, diff_lambda_q2, diff_lambda_k2, diff_subln_g, w_out, norm2_g,
           w_gate, w_up, w_down, final_g):
    assert x.shape[0] == 1 and ctx.shape[0] == 1
    depth = w_ada.shape[0]
    xl, xc = x[0], ctx[0]
    n_lat, d = xl.shape
    n_ctx = xc.shape[0]
    total = n_lat + n_ctx
    dq = d // 4

    mods = _ada(jnp.concatenate([c, c_ctx[None, :]], axis=0).T, w_ada, b_ada)
    mods = mods.reshape(depth, 2, N_MOD, 1, d)
    tabs_lat = _rope_tables(n_lat, identity=False)
    tabs_ctx = _rope_tables(n_ctx, identity=True)
    w_in_b, w_out_b = w_in.astype(BF16), w_out.astype(BF16)
    w_gate_b, w_up_b, w_down_b = w_gate.astype(BF16), w_up.astype(BF16), w_down.astype(BF16)
    sgu_w_b = sgu_w.astype(BF16)
    sgu_b_col = sgu_b[..., None]
    fg = final_g[None, :]

    for li in range(depth):
        need_ctx = li < depth - 1
        lam_init = 0.8 - 0.6 * math.exp(-0.3 * li)
        g1, g2 = norm1_g[li][None, :], norm2_g[li][None, :]
        p = _inproj(xl, mods, li, 0, g1, w_in_b[li], tabs_lat, total, 0)
        p = _inproj(xc, mods, li, 1, g1, w_in_b[li], tabs_ctx, total, n_lat, p_prev=p)

        dec = jnp.stack([ret_decay_fwd[li], ret_decay_bwd[li]]).astype(F32)
        o_f, o_b = _retention(p, dec, n_lat, n_ctx, dq)
        y_a, y_b = _mix(p, o_f, o_b, sgu_w_b[li], sgu_b_col[li], dq)
        lam_vecs = jnp.stack([diff_lambda_q1[li], diff_lambda_k1[li],
                              diff_lambda_q2[li], diff_lambda_k2[li]]).astype(F32)
        sub_g = diff_subln_g[li][None, :]
        y_c = _attention(p, lam_vecs, sub_g, 0, n_lat, 0, total, d, lam_init)
        xl1 = _outproj(y_a, y_b, y_c, 0, w_out_b[li], xl, mods, li, 0)
        last = li == depth - 1
        xl_new = _ffn(xl1, mods, li, 0, g2, w_gate_b[li], w_up_b[li], w_down_b[li], fg, last)
        if need_ctx:
            yc_c = _attention(p, lam_vecs, sub_g, n_lat, n_ctx, n_lat, n_ctx, d, lam_init)
            xc1 = _outproj(y_a, y_b, yc_c, n_lat, w_out_b[li], xc, mods, li, 1)
            xc = _ffn(xc1, mods, li, 1, g2, w_gate_b[li], w_up_b[li], w_down_b[li], fg, False)
        xl = xl_new
    return xl[None]
```

```python
import functools
import math

import jax
import jax.numpy as jnp
from jax import lax
from jax.experimental import pallas as pl
from jax.experimental.pallas import tpu as pltpu

GRID_W = 64
ROPE_BASE = 10000.0
EPS = 1e-6
N_MOD = 6
HEAD = 128
SGU_CHUNK = 128
RET_BLOCK = 256
IN_TILES = 12
VMEM_LIMIT_BYTES = 62 * 1024 * 1024

F32 = jnp.float32
BF16 = jnp.bfloat16


def _cparams(*sem):
    return pltpu.CompilerParams(dimension_semantics=sem, vmem_limit_bytes=VMEM_LIMIT_BYTES)


def _tile(n, want):
    if n <= want:
        return n
    t = want
    while n % t:
        t -= 8
    return t


def _ada_kernel(a_ref, w_ref, b_ref, o_ref, *, rows):
    d, tn = w_ref.shape[1], w_ref.shape[2]

    def body(r, acc):
        r0 = pl.multiple_of(r * rows, rows)
        a = a_ref[pl.ds(r0, rows), :]
        a = a * jax.nn.sigmoid(a)
        w = w_ref[0, pl.ds(r0, rows), :]
        acc0, acc1 = acc
        acc0 = acc0 + (w * a[:, 0:1]).reshape(rows // 8, 8, tn).sum(axis=0)
        acc1 = acc1 + (w * a[:, 1:2]).reshape(rows // 8, 8, tn).sum(axis=0)
        return acc0, acc1

    z = jnp.zeros((8, tn), F32)
    acc0, acc1 = lax.fori_loop(0, d // rows, body, (z, z))
    out = jnp.concatenate([acc0.sum(axis=0, keepdims=True), acc1.sum(axis=0, keepdims=True)], axis=0)
    o_ref[0] = out + b_ref[0]


def _ada(c2t, w_ada, b_ada):
    depth, d, n = w_ada.shape
    tn = _tile(n, 1024)
    return pl.pallas_call(
        functools.partial(_ada_kernel, rows=64),
        out_shape=jax.ShapeDtypeStruct((depth, 2, n), F32),
        grid=(depth, n // tn),
        in_specs=[
            pl.BlockSpec((d, 2), lambda l, j: (0, 0)),
            pl.BlockSpec((1, d, tn), lambda l, j: (l, 0, j)),
            pl.BlockSpec((1, 1, tn), lambda l, j: (l, 0, j)),
        ],
        out_specs=pl.BlockSpec((1, 2, tn), lambda l, j: (l, 0, j)),
        compiler_params=_cparams("parallel", "parallel"),
        name="ada",
    )(c2t, w_ada, b_ada.reshape(depth, 1, n))


def _rope(x, c, s1, s2, shift):
    return x * c + pltpu.roll(x, HEAD - shift, 1) * s1 + pltpu.roll(x, shift, 1) * s2


def _group_rms(x):
    return x * lax.rsqrt(jnp.mean(x * x, axis=-1, keepdims=True) + EPS)


def _modulated_norm(x, g, sh, sc):
    y = x * lax.rsqrt(jnp.mean(x * x, axis=-1, keepdims=True) + EPS) * g
    return y * (1.0 + sc) + sh


def _inproj_kernel(x_ref, sh_ref, sc_ref, g_ref, w_ref, tab_ref, *rest, aliased):
    o_ref, h_ref = rest[1:] if aliased else rest
    j = pl.program_id(1)
    tn = w_ref.shape[1]
    slab = min(2 * HEAD, tn)

    @pl.when(j == 0)
    def _():
        h_ref[...] = _modulated_norm(x_ref[...], g_ref[...], sh_ref[...], sc_ref[...]).astype(BF16)

    def project(fn):
        for sb in range(tn // slab):
            acc = jnp.dot(h_ref[...], w_ref[:, sb * slab:(sb + 1) * slab], preferred_element_type=F32)
            for hh in range(slab // HEAD):
                c0 = sb * slab + hh * HEAD
                o_ref[:, c0:c0 + HEAD] = fn(acc[:, hh * HEAD:(hh + 1) * HEAD]).astype(BF16)

    def rope_ret(scale):
        c, s1, s2 = tab_ref[0], tab_ref[1], tab_ref[2]
        return lambda a: _rope(a, c, s1, s2, HEAD // 4) * scale

    def rope_diff(scale):
        c, s1, s2 = tab_ref[3], tab_ref[4], tab_ref[5]
        return lambda a: _rope(a, c, s1, s2, HEAD // 8) * scale

    @pl.when(j == 0)
    def _():
        project(jax.nn.gelu)

    @pl.when(j == 1)
    def _():
        project(lambda a: _group_rms(jax.nn.gelu(a)))

    @pl.when(j == 2)
    def _():
        project(rope_ret(1.0))

    @pl.when(j == 3)
    def _():
        project(rope_ret(HEAD ** -0.5))

    @pl.when((j == 4) | (j >= 10))
    def _():
        project(lambda a: a)

    @pl.when(j == 5)
    def _():
        project(lambda a: a * jax.nn.sigmoid(a))

    @pl.when((j == 6) | (j == 7))
    def _():
        project(rope_diff((HEAD // 2) ** -0.5 * math.log2(math.e)))

    @pl.when((j == 8) | (j == 9))
    def _():
        project(rope_diff(1.0))


def _inproj(x, mods, li, row, g, w, tabs, total_rows, row_off, p_prev=None):
    m, d = x.shape
    n = w.shape[2]
    tn = n // IN_TILES
    tm = _tile(m, 1024)
    assert row_off % tm == 0
    ob = row_off // tm
    aliased = p_prev is not None
    in_specs = [
        pl.BlockSpec((tm, d), lambda i, j: (i, 0)),
        pl.BlockSpec((None, None, None, 1, d), lambda i, j: (li, row, 0, 0, 0)),
        pl.BlockSpec((None, None, None, 1, d), lambda i, j: (li, row, 1, 0, 0)),
        pl.BlockSpec((None, 1, d), lambda i, j: (li, 0, 0)),
        pl.BlockSpec((None, d, tn), lambda i, j: (li, 0, j)),
        pl.BlockSpec((6, tm, HEAD), lambda i, j: (0, i, 0)),
    ]
    args = [x, mods, mods, g, w, tabs]
    if aliased:
        in_specs.append(pl.BlockSpec(memory_space=pl.ANY))
        args.append(p_prev)
    return pl.pallas_call(
        functools.partial(_inproj_kernel, aliased=aliased),
        out_shape=jax.ShapeDtypeStruct((total_rows, n), BF16),
        grid=(m // tm, IN_TILES),
        in_specs=in_specs,
        out_specs=pl.BlockSpec((tm, tn), lambda i, j: (ob + i, j)),
        scratch_shapes=[pltpu.VMEM((tm, d), BF16)],
        input_output_aliases={6: 0} if aliased else {},
        compiler_params=_cparams("parallel", "arbitrary"),
        name="inproj",
    )(*args)


def _log_sigmoid(z):
    return jnp.minimum(z, 0.0) - jnp.log1p(jnp.exp(-jnp.abs(z)))


def _ret_kernel(dec_ref, qf_ref, kf_ref, vf_ref, qb_ref, kb_ref, vb_ref, of_ref, ob_ref,
                state_ref, mask_ref, qd_ref, kd_ref, cd_ref, *, li):
    s = pl.program_id(0)
    rc = qf_ref.shape[0]
    nh = qf_ref.shape[1] // HEAD

    @pl.when(s == 0)
    def _():
        state_ref[...] = jnp.zeros_like(state_ref)
        ii = lax.broadcasted_iota(jnp.int32, (rc, rc), 0).astype(F32)
        jj = lax.broadcasted_iota(jnp.int32, (rc, rc), 1).astype(F32)
        col = lax.broadcasted_iota(jnp.int32, (rc, HEAD), 0).astype(F32)
        for dr in range(2):
            rel = (ii - jj) if dr == 0 else (jj - ii)
            for h in range(nh):
                logit = dec_ref[dr, li, h]
                lg = _log_sigmoid(jnp.full((rc, rc), logit, F32))
                mask_ref[dr, h] = jnp.where(rel >= 0, jnp.exp(jnp.maximum(rel, 0.0) * lg), 0.0)
                lgc = _log_sigmoid(jnp.full((rc, HEAD), logit, F32))
                q_pow = (col + 1.0) if dr == 0 else (rc - col)
                k_pow = (rc - 1.0 - col) if dr == 0 else col
                qd_ref[dr, h] = jnp.exp(q_pow * lgc)
                kd_ref[dr, h] = jnp.exp(k_pow * lgc)
                cd_ref[dr, h] = jnp.exp(rc * _log_sigmoid(jnp.full((HEAD, HEAD), logit, F32)))

    for dr, (q_ref, k_ref, v_ref, o_ref) in enumerate(
            ((qf_ref, kf_ref, vf_ref, of_ref), (qb_ref, kb_ref, vb_ref, ob_ref))):
        for h in range(nh):
            sl = slice(h * HEAD, (h + 1) * HEAD)
            q = q_ref[:, sl]
            k = k_ref[:, sl]
            v = v_ref[:, sl]
            state = state_ref[dr, h]
            scores = lax.dot_general(q, k, (((1,), (1,)), ((), ())), preferred_element_type=F32)
            scores = (scores * mask_ref[dr, h]).astype(BF16)
            qs = (q.astype(F32) * qd_ref[dr, h]).astype(BF16)
            out = jnp.dot(scores, v, preferred_element_type=F32)
            out = out + jnp.dot(qs, state.astype(BF16), preferred_element_type=F32)
            o_ref[:, sl] = out
            ks = (k.astype(F32) * kd_ref[dr, h]).astype(BF16)
            upd = lax.dot_general(ks, v, (((0,), (0,)), ((), ())), preferred_element_type=F32)
            state_ref[dr, h] = state * cd_ref[dr, h] + upd


def _retention(p, dec, li, n_lat, n_ctx, d_ret):
    t = p.shape[0]
    rc = RET_BLOCK
    assert n_lat % rc == 0 and n_ctx % rc == 0
    nb, nlb, ncb = t // rc, n_lat // rc, n_ctx // rc
    nh = d_ret // HEAD
    cq, ck, cv = 2, 3, 4

    def fmap(col):
        return lambda s: (jnp.where(s < ncb, nlb + s, s - ncb), col)

    def bmap(col):
        return lambda s: (nb - 1 - s, col)

    blk = lambda im: pl.BlockSpec((rc, d_ret), im)
    return pl.pallas_call(
        functools.partial(_ret_kernel, li=li),
        out_shape=[jax.ShapeDtypeStruct((t, d_ret), F32)] * 2,
        grid=(nb,),
        in_specs=[pl.BlockSpec(memory_space=pltpu.SMEM),
                  blk(fmap(cq)), blk(fmap(ck)), blk(fmap(cv)),
                  blk(bmap(cq)), blk(bmap(ck)), blk(bmap(cv))],
        out_specs=[blk(fmap(0)), blk(bmap(0))],
        scratch_shapes=[
            pltpu.VMEM((2, nh, HEAD, HEAD), F32),
            pltpu.VMEM((2, nh, rc, rc), F32),
            pltpu.VMEM((2, nh, rc, HEAD), F32),
            pltpu.VMEM((2, nh, rc, HEAD), F32),
            pltpu.VMEM((2, nh, HEAD, HEAD), F32),
        ],
        compiler_params=_cparams("arbitrary"),
        name="retention",
    )(dec, p, p, p, p, p, p)


def _mix_kernel(u_ref, v_ref, gate_ref, of_ref, ob_ref, ws_ref, bs_ref, ya_ref, yb_ref):
    tm = u_ref.shape[0]
    ng = u_ref.shape[1] // HEAD
    for n in range(tm // SGU_CHUNK):
        rs = slice(n * SGU_CHUNK, (n + 1) * SGU_CHUNK)
        for g in range(ng):
            cs = slice(g * HEAD, (g + 1) * HEAD)
            mixed = jnp.dot(ws_ref[g], v_ref[rs, cs], preferred_element_type=F32) + bs_ref[g]
            ya_ref[rs, cs] = (u_ref[rs, cs].astype(F32) * mixed).astype(BF16)
    for h in range(of_ref.shape[1] // HEAD):
        cs = slice(h * HEAD, (h + 1) * HEAD)
        o = of_ref[:, cs] + ob_ref[:, cs]
        yb_ref[:, cs] = (_group_rms(o) * gate_ref[:, cs].astype(F32)).astype(BF16)


def _mix(p, o_f, o_b, ws, bs, li, dq):
    t = p.shape[0]
    tm = _tile(t, 256)
    row = lambda col: pl.BlockSpec((tm, dq), lambda i: (i, col))
    full = lambda a: pl.BlockSpec((None,) + a.shape[1:], lambda i: (li,) + (0,) * (a.ndim - 1))
    return pl.pallas_call(
        _mix_kernel,
        out_shape=[jax.ShapeDtypeStruct((t, dq), BF16)] * 2,
        grid=(t // tm,),
        in_specs=[row(0), row(1), row(5), row(0), row(0), full(ws), full(bs)],
        out_specs=[row(0), row(0)],
        compiler_params=_cparams("parallel"),
        name="mix",
    )(p, p, p, o_f, o_b, ws, bs)


def _attn_kernel(q_ref, k_ref, v_ref, lam_ref, g_ref, o_ref,
                 s0_ref, s1_ref, p0_ref, p1_ref, a0_ref, a1_ref, acc_ref, m_ref, qs_ref, *, tk, lam_init):
    tq = q_ref.shape[0]
    nkv = k_ref.shape[0] // tk
    s_bufs, p_bufs, a_bufs = (s0_ref, s1_ref), (p0_ref, p1_ref), (a0_ref, a1_ref)
    q = q_ref[...]
    lane = lax.broadcasted_iota(jnp.int32, q.shape, 1)
    zero = jnp.zeros_like(q)
    qs_ref[:tq] = jnp.where(lane < HEAD // 2, q, zero)
    qs_ref[tq:] = jnp.where(lane >= HEAD // 2, q, zero)
    acc_ref[...] = jnp.zeros_like(acc_ref)
    m_ref[...] = jnp.full(m_ref.shape, -jnp.inf, F32)

    def rows(c):
        r0 = c * tk
        return pl.ds(r0 if isinstance(r0, int) else pl.multiple_of(r0, tk), tk)

    def scores(c, par):
        s_bufs[par][...] = lax.dot_general(qs_ref[...], k_ref[rows(c), :], (((1,), (1,)), ((), ())),
                                           preferred_element_type=F32)

    def softmax(par):
        s_ref = s_bufs[par]
        m_old = m_ref[...]
        m_new = jnp.maximum(m_old, jnp.broadcast_to(s_ref[...].max(axis=1, keepdims=True), m_old.shape))
        a_bufs[par][...] = jnp.exp2(m_old - m_new)
        m_ref[...] = m_new
        for c in range(tk // HEAD):
            sl = slice(c * HEAD, (c + 1) * HEAD)
            p_bufs[par][:, sl] = jnp.exp2(s_ref[:, sl] - m_new).astype(BF16)

    def weighted_values(c, par):
        v_ext = jnp.concatenate([v_ref[rows(c), :], jnp.ones((tk, HEAD), BF16)], axis=1)
        pv = jnp.dot(p_bufs[par][...], v_ext, preferred_element_type=F32)
        alpha = a_bufs[par][...]
        acc_ref[:, :HEAD] = alpha * acc_ref[:, :HEAD] + pv[:, :HEAD]
        acc_ref[:, HEAD:] = alpha * acc_ref[:, HEAD:] + pv[:, HEAD:]

    def step(t, par, do_scores=True, do_softmax=True, do_values=True):
        if do_scores:
            scores(t, par)
        if do_softmax:
            softmax(1 - par)
        if do_values:
            weighted_values(t - 2, par)

    t = 0
    while t < min(2, nkv + 2):
        step(t, t % 2, do_scores=t < nkv, do_softmax=1 <= t <= nkv, do_values=False)
        t += 1
    n_pairs = max(0, (nkv - t) // 2)
    if n_pairs:
        t0 = t

        def pair(i, carry):
            tt = t0 + 2 * i
            step(tt, t0 % 2)
            step(tt + 1, (t0 + 1) % 2)
            return carry

        lax.fori_loop(0, n_pairs, pair, 0)
        t += 2 * n_pairs
    while t < nkv + 2:
        step(t, t % 2, do_scores=t < nkv, do_softmax=1 <= t <= nkv, do_values=t >= 2)
        t += 1

    lv = lam_ref[...]
    lam = (jnp.exp(jnp.sum(lv[0:1] * lv[1:2], axis=1, keepdims=True))
           - jnp.exp(jnp.sum(lv[2:3] * lv[3:4], axis=1, keepdims=True)) + lam_init)
    on = acc_ref[:, :HEAD] / acc_ref[:, HEAD:]
    o = on[:tq] - lam * on[tq:]
    y = o * lax.rsqrt(jnp.mean(o * o, axis=-1, keepdims=True) + EPS) * g_ref[...]
    o_ref[...] = (y * (1.0 - lam_init)).astype(BF16)


def _attention(p, lam_vecs, subln_g, li, q_row0, n_q, k_row0, n_k, d_model, lam_init):
    d_diff = d_model // 2
    nh = d_diff // HEAD
    quarter = (d_model // 4) // HEAD
    cq, ck, cv = 6 * quarter, 8 * quarter, 10 * quarter
    tq = _tile(n_q, 512)
    tk = _tile(n_k, 768)
    assert q_row0 % tq == 0 and k_row0 % n_k == 0
    qb, kb = q_row0 // tq, k_row0 // n_k
    return pl.pallas_call(
        functools.partial(_attn_kernel, tk=tk, lam_init=lam_init),
        out_shape=jax.ShapeDtypeStruct((n_q, d_diff), BF16),
        grid=(nh, n_q // tq),
        in_specs=[
            pl.BlockSpec((tq, HEAD), lambda h, i: (qb + i, cq + h)),
            pl.BlockSpec((n_k, HEAD), lambda h, i: (kb, ck + h)),
            pl.BlockSpec((n_k, HEAD), lambda h, i: (kb, cv + h)),
            pl.BlockSpec((None,) + lam_vecs.shape[1:], lambda h, i: (li, 0, 0)),
            pl.BlockSpec((None, 1, HEAD), lambda h, i: (li, 0, 0)),
        ],
        out_specs=pl.BlockSpec((tq, HEAD), lambda h, i: (i, h)),
        scratch_shapes=[pltpu.VMEM((2 * tq, tk), F32), pltpu.VMEM((2 * tq, tk), F32),
                        pltpu.VMEM((2 * tq, tk), BF16), pltpu.VMEM((2 * tq, tk), BF16),
                        pltpu.VMEM((2 * tq, HEAD), F32), pltpu.VMEM((2 * tq, HEAD), F32),
                        pltpu.VMEM((2 * tq, 2 * HEAD), F32), pltpu.VMEM((2 * tq, HEAD), F32),
                        pltpu.VMEM((2 * tq, HEAD), BF16)],
        compiler_params=_cparams("parallel", "parallel"),
        name="diffattn",
    )(p, p, p, lam_vecs, subln_g)


def _outproj_kernel(ya_ref, yb_ref, yc_ref, wa_ref, wb_ref, wc_ref, x_ref, gate_ref, o_ref):
    acc = jnp.dot(ya_ref[...], wa_ref[...], preferred_element_type=F32)
    acc = acc + jnp.dot(yb_ref[...], wb_ref[...], preferred_element_type=F32)
    acc = acc + jnp.dot(yc_ref[...], wc_ref[...], preferred_element_type=F32)
    o_ref[...] = x_ref[...] + gate_ref[...] * acc


def _outproj(ya, yb, yc, y_row0, w, x, mods, li, row):
    m, d = x.shape
    dq = d // 4
    tm = _tile(m, 512)
    assert y_row0 % tm == 0
    yb0 = y_row0 // tm
    return pl.pallas_call(
        _outproj_kernel,
        out_shape=jax.ShapeDtypeStruct((m, d), F32),
        grid=(m // tm,),
        in_specs=[
            pl.BlockSpec((tm, dq), lambda i: (yb0 + i, 0)),
            pl.BlockSpec((tm, dq), lambda i: (yb0 + i, 0)),
            pl.BlockSpec((tm, 2 * dq), lambda i: (i, 0)),
            pl.BlockSpec((None, dq, d), lambda i: (li, 0, 0)),
            pl.BlockSpec((None, dq, d), lambda i: (li, 1, 0)),
            pl.BlockSpec((None, 2 * dq, d), lambda i: (li, 1, 0)),
            pl.BlockSpec((tm, d), lambda i: (i, 0)),
            pl.BlockSpec((None, None, None, 1, d), lambda i: (li, row, 2, 0, 0)),
        ],
        out_specs=pl.BlockSpec((tm, d), lambda i: (i, 0)),
        compiler_params=_cparams("parallel"),
        name="outproj",
    )(ya, yb, yc, w, w, w, x, mods)


def _ffn_kernel(x_ref, sh_ref, sc_ref, gate_ref, g_ref, wg_ref, wu_ref, wd_ref, fg_ref, o_ref, h_ref,
                *, final_norm):
    f = pl.program_id(1)

    @pl.when(f == 0)
    def _():
        h_ref[...] = _modulated_norm(x_ref[...], g_ref[...], sh_ref[...], sc_ref[...]).astype(BF16)
        o_ref[...] = jnp.zeros_like(o_ref)

    h = h_ref[...]
    a = jnp.dot(h, wg_ref[...], preferred_element_type=F32)
    b = jnp.dot(h, wu_ref[...], preferred_element_type=F32)
    act = (a * jax.nn.sigmoid(a) * b).astype(BF16)
    o_ref[...] += jnp.dot(act, wd_ref[...], preferred_element_type=F32)

    @pl.when(f == pl.num_programs(1) - 1)
    def _():
        y = x_ref[...] + gate_ref[...] * o_ref[...]
        if final_norm:
            y = y * lax.rsqrt(jnp.mean(y * y, axis=-1, keepdims=True) + EPS) * fg_ref[...]
        o_ref[...] = y


def _ffn(x, mods, li, row, g, wg, wu, wd, final_g, final_norm):
    m, d = x.shape
    ff = wg.shape[2]
    tm = _tile(m, 1024)
    tf = _tile(ff, 256)
    mod = lambda c: pl.BlockSpec((None, None, None, 1, d), lambda i, f: (li, row, c, 0, 0))
    return pl.pallas_call(
        functools.partial(_ffn_kernel, final_norm=final_norm),
        out_shape=jax.ShapeDtypeStruct((m, d), F32),
        grid=(m // tm, ff // tf),
        in_specs=[
            pl.BlockSpec((tm, d), lambda i, f: (i, 0)),
            mod(3), mod(4), mod(5),
            pl.BlockSpec((None, 1, d), lambda i, f: (li, 0, 0)),
            pl.BlockSpec((None, d, tf), lambda i, f: (li, 0, f)),
            pl.BlockSpec((None, d, tf), lambda i, f: (li, 0, f)),
            pl.BlockSpec((None, tf, d), lambda i, f: (li, f, 0)),
            pl.BlockSpec((1, d), lambda i, f: (0, 0)),
        ],
        out_specs=pl.BlockSpec((tm, d), lambda i, f: (i, 0)),
        scratch_shapes=[pltpu.VMEM((tm, d), BF16)],
        compiler_params=_cparams("parallel", "arbitrary"),
        name="ffn",
    )(x, mods, mods, mods, g, wg, wu, wd, final_g)


def _rope_tables(n_rows, identity):
    def one(width):
        half = width // 2
        lane = jnp.arange(HEAD)
        sub = lane // width
        idx = lane % width
        first = idx < half
        freqs = ROPE_BASE ** (-(idx % half).astype(F32) / half)
        t = jnp.arange(n_rows)
        pos = jnp.where((sub % 2) == 0, (t // GRID_W)[:, None], (t % GRID_W)[:, None]).astype(F32)
        ang = pos * freqs[None, :]
        cos, sin = jnp.cos(ang), jnp.sin(ang)
        if identity:
            cos, sin = jnp.ones_like(cos), jnp.zeros_like(sin)
        return [cos, jnp.where(first[None, :], -sin, 0.0), jnp.where(first[None, :], 0.0, sin)]
    return jnp.stack(one(HEAD // 2) + one(HEAD // 4)).astype(F32)


def kernel(x, c, ctx, c_ctx, w_ada, b_ada, norm1_g, w_in, sgu_w, sgu_b, ret_decay_fwd, ret_decay_bwd,
           diff_lambda_q1, diff_lambda_k1, diff_lambda_q2, diff_lambda_k2, diff_subln_g, w_out, norm2_g,
           w_gate, w_up, w_down, final_g):
    assert x.shape[0] == 1 and ctx.shape[0] == 1
    depth = w_ada.shape[0]
    xl, xc = x[0], ctx[0]
    n_lat, d = xl.shape
    n_ctx = xc.shape[0]
    total = n_lat + n_ctx
    dq = d // 4

    mods = _ada(jnp.concatenate([c, c_ctx[None, :]], axis=0).T, w_ada, b_ada)
    mods = mods.reshape(depth, 2, N_MOD, 1, d)
    tabs_lat = _rope_tables(n_lat, identity=False)
    tabs_ctx = _rope_tables(n_ctx, identity=True)
    w_in_b, w_out_b = w_in.astype(BF16), w_out.astype(BF16)
    w_gate_b, w_up_b, w_down_b = w_gate.astype(BF16), w_up.astype(BF16), w_down.astype(BF16)
    sgu_w_b = sgu_w.astype(BF16)
    sgu_b_col = sgu_b[..., None]
    g1, g2, fg = norm1_g[:, None, :], norm2_g[:, None, :], final_g[None, :]
    dec = jnp.stack([ret_decay_fwd, ret_decay_bwd]).astype(F32)
    lam_vecs = jnp.stack([diff_lambda_q1, diff_lambda_k1, diff_lambda_q2, diff_lambda_k2], axis=1).astype(F32)
    sub_g = diff_subln_g[:, None, :]

    for li in range(depth):
        need_ctx = li < depth - 1
        last = li == depth - 1
        lam_init = 0.8 - 0.6 * math.exp(-0.3 * li)
        p = _inproj(xl, mods, li, 0, g1, w_in_b, tabs_lat, total, 0)
        p = _inproj(xc, mods, li, 1, g1, w_in_b, tabs_ctx, total, n_lat, p_prev=p)
        o_f, o_b = _retention(p, dec, li, n_lat, n_ctx, dq)
        y_a, y_b = _mix(p, o_f, o_b, sgu_w_b, sgu_b_col, li, dq)
        y_c = _attention(p, lam_vecs, sub_g, li, 0, n_lat, 0, total, d, lam_init)
        xl1 = _outproj(y_a, y_b, y_c, 0, w_out_b, xl, mods, li, 0)
        xl_new = _ffn(xl1, mods, li, 0, g2, w_gate_b, w_up_b, w_down_b, fg, last)
        if need_ctx:
            yc_c = _attention(p, lam_vecs, sub_g, li, n_lat, n_ctx, n_lat, n_ctx, d, lam_init)
            xc1 = _outproj(y_a, y_b, yc_c, n_lat, w_out_b, xc, mods, li, 1)
            xc = _ffn(xc1, mods, li, 1, g2, w_gate_b, w_up_b, w_down_b, fg, False)
        xl = xl_new
    return xl[None]
```

```python
import functools
import math

import jax
import jax.numpy as jnp
from jax import lax
from jax.experimental import pallas as pl
from jax.experimental.pallas import tpu as pltpu

GRID_W = 64
ROPE_BASE = 10000.0
EPS = 1e-6
N_MOD = 6
HEAD = 128
SGU_CHUNK = 128
RET_BLOCK = 256
IN_TILES = 12
ROW_CHUNK = 256
VMEM_LIMIT_BYTES = 62 * 1024 * 1024

F32 = jnp.float32
BF16 = jnp.bfloat16


def _cparams(*sem):
    return pltpu.CompilerParams(dimension_semantics=sem, vmem_limit_bytes=VMEM_LIMIT_BYTES)


def _tile(n, want):
    if n <= want:
        return n
    t = want
    while n % t:
        t -= 8
    return t


def _ada_kernel(a_ref, w_ref, b_ref, o_ref, ab_ref, *, rows):
    d, tn = w_ref.shape[1], w_ref.shape[2]

    @pl.when((pl.program_id(0) == 0) & (pl.program_id(1) == 0))
    def _():
        a = a_ref[...]
        a = a * jax.nn.sigmoid(a)
        ab_ref[0] = jnp.broadcast_to(a[:, 0:1], (d, HEAD))
        ab_ref[1] = jnp.broadcast_to(a[:, 1:2], (d, HEAD))

    def body(r, acc):
        r0 = pl.multiple_of(r * rows, rows)
        a0 = ab_ref[0, pl.ds(r0, rows), :]
        a1 = ab_ref[1, pl.ds(r0, rows), :]
        new = []
        for c in range(tn // HEAD):
            w = w_ref[0, pl.ds(r0, rows), c * HEAD:(c + 1) * HEAD]
            new.append(acc[2 * c] + (w * a0).reshape(rows // 8, 8, HEAD).sum(axis=0))
            new.append(acc[2 * c + 1] + (w * a1).reshape(rows // 8, 8, HEAD).sum(axis=0))
        return tuple(new)

    z = jnp.zeros((8, HEAD), F32)
    acc = lax.fori_loop(0, d // rows, body, (z,) * (2 * (tn // HEAD)))
    for c in range(tn // HEAD):
        out = jnp.concatenate([acc[2 * c].sum(axis=0, keepdims=True), acc[2 * c + 1].sum(axis=0, keepdims=True)],
                              axis=0)
        o_ref[0, :, c * HEAD:(c + 1) * HEAD] = out + b_ref[0, :, c * HEAD:(c + 1) * HEAD]


def _ada(c2t, w_ada, b_ada):
    depth, d, n = w_ada.shape
    tn = _tile(n, 1024)
    return pl.pallas_call(
        functools.partial(_ada_kernel, rows=128),
        out_shape=jax.ShapeDtypeStruct((depth, 2, n), F32),
        grid=(depth, n // tn),
        in_specs=[
            pl.BlockSpec((d, 2), lambda l, j: (0, 0)),
            pl.BlockSpec((1, d, tn), lambda l, j: (l, 0, j)),
            pl.BlockSpec((1, 1, tn), lambda l, j: (l, 0, j)),
        ],
        out_specs=pl.BlockSpec((1, 2, tn), lambda l, j: (l, 0, j)),
        scratch_shapes=[pltpu.VMEM((2, d, HEAD), F32)],
        compiler_params=_cparams("arbitrary", "arbitrary"),
        name="ada",
    )(c2t, w_ada, b_ada.reshape(depth, 1, n))


def _rope(x, c, s1, s2, shift):
    return x * c + pltpu.roll(x, HEAD - shift, 1) * s1 + pltpu.roll(x, shift, 1) * s2


def _gelu_tanh(x):
    c = 2.0 * math.sqrt(2.0 / math.pi)
    return x * jax.nn.sigmoid(x * (c + (c * 0.044715) * (x * x)))


def _group_rms(x):
    return x * lax.rsqrt(jnp.mean(x * x, axis=-1, keepdims=True) + EPS)


def _modulated_norm(x, g, sh, sc):
    y = x * lax.rsqrt(jnp.mean(x * x, axis=-1, keepdims=True) + EPS) * g
    return y * (1.0 + sc) + sh


def _inproj_kernel(x_ref, sh_ref, sc_ref, g_ref, w_ref, tab_ref, *rest, aliased):
    o_ref, h_ref = rest[1:] if aliased else rest
    j = pl.program_id(1)
    tn = w_ref.shape[1]
    slab = min(2 * HEAD, tn)

    @pl.when(j == 0)
    def _():
        h_ref[...] = _modulated_norm(x_ref[...], g_ref[...], sh_ref[...], sc_ref[...]).astype(BF16)

    def project(fn):
        rc = min(h_ref.shape[0], ROW_CHUNK)
        for sb in range(tn // slab):
            w = w_ref[:, sb * slab:(sb + 1) * slab].astype(BF16)
            for r in range(h_ref.shape[0] // rc):
                rs = slice(r * rc, (r + 1) * rc)
                acc = jnp.dot(h_ref[rs, :], w, preferred_element_type=F32)
                for hh in range(slab // HEAD):
                    c0 = sb * slab + hh * HEAD
                    o_ref[rs, c0:c0 + HEAD] = fn(acc[:, hh * HEAD:(hh + 1) * HEAD], rs).astype(BF16)

    def rope_ret(scale):
        return lambda a, rs: _rope(a, tab_ref[0, rs, :], tab_ref[1, rs, :], tab_ref[2, rs, :], HEAD // 4) * scale

    def rope_diff(scale):
        return lambda a, rs: _rope(a, tab_ref[3, rs, :], tab_ref[4, rs, :], tab_ref[5, rs, :], HEAD // 8) * scale

    @pl.when(j == 0)
    def _():
        project(lambda a, rs: _gelu_tanh(a))

    @pl.when(j == 1)
    def _():
        project(lambda a, rs: _group_rms(_gelu_tanh(a)))

    @pl.when(j == 2)
    def _():
        project(rope_ret(1.0))

    @pl.when(j == 3)
    def _():
        project(rope_ret(HEAD ** -0.5))

    @pl.when((j == 4) | (j >= 10))
    def _():
        project(lambda a, rs: a)

    @pl.when(j == 5)
    def _():
        project(lambda a, rs: a * jax.nn.sigmoid(a))

    @pl.when((j == 6) | (j == 7))
    def _():
        project(rope_diff((HEAD // 2) ** -0.5 * math.log2(math.e)))

    @pl.when((j == 8) | (j == 9))
    def _():
        project(rope_diff(1.0))


def _inproj(x, mods, li, row, g, w, tabs, total_rows, row_off, p_prev=None):
    m, d = x.shape
    n = w.shape[2]
    tn = n // IN_TILES
    tm = _tile(m, 1024)
    assert row_off % tm == 0
    ob = row_off // tm
    aliased = p_prev is not None
    in_specs = [
        pl.BlockSpec((tm, d), lambda i, j: (i, 0)),
        pl.BlockSpec((None, None, None, 1, d), lambda i, j: (li, row, 0, 0, 0)),
        pl.BlockSpec((None, None, None, 1, d), lambda i, j: (li, row, 1, 0, 0)),
        pl.BlockSpec((None, 1, d), lambda i, j: (li, 0, 0)),
        pl.BlockSpec((None, d, tn), lambda i, j: (li, 0, j)),
        pl.BlockSpec((6, tm, HEAD), lambda i, j: (0, i, 0)),
    ]
    args = [x, mods, mods, g, w, tabs]
    if aliased:
        in_specs.append(pl.BlockSpec(memory_space=pl.ANY))
        args.append(p_prev)
    return pl.pallas_call(
        functools.partial(_inproj_kernel, aliased=aliased),
        out_shape=jax.ShapeDtypeStruct((total_rows, n), BF16),
        grid=(m // tm, IN_TILES),
        in_specs=in_specs,
        out_specs=pl.BlockSpec((tm, tn), lambda i, j: (ob + i, j)),
        scratch_shapes=[pltpu.VMEM((tm, d), BF16)],
        input_output_aliases={6: 0} if aliased else {},
        compiler_params=_cparams("parallel", "arbitrary"),
        name="inproj",
    )(*args)


def _log_sigmoid(z):
    return jnp.minimum(z, 0.0) - jnp.log1p(jnp.exp(-jnp.abs(z)))


def _ret_kernel(dec_ref, qf_ref, kf_ref, vf_ref, qb_ref, kb_ref, vb_ref, of_ref, ob_ref,
                state_ref, mask_ref, qd_ref, kd_ref, cd_ref, *, li):
    s = pl.program_id(0)
    rc = qf_ref.shape[0]
    nh = qf_ref.shape[1] // HEAD

    @pl.when(s == 0)
    def _():
        state_ref[...] = jnp.zeros_like(state_ref)
        ii = lax.broadcasted_iota(jnp.int32, (rc, rc), 0).astype(F32)
        jj = lax.broadcasted_iota(jnp.int32, (rc, rc), 1).astype(F32)
        col = lax.broadcasted_iota(jnp.int32, (rc, HEAD), 0).astype(F32)
        for dr in range(2):
            rel = (ii - jj) if dr == 0 else (jj - ii)
            for h in range(nh):
                logit = dec_ref[dr, li, h]
                lg = _log_sigmoid(jnp.full((rc, rc), logit, F32))
                mask_ref[dr, h] = jnp.where(rel >= 0, jnp.exp(jnp.maximum(rel, 0.0) * lg), 0.0)
                lgc = _log_sigmoid(jnp.full((rc, HEAD), logit, F32))
                q_pow = (col + 1.0) if dr == 0 else (rc - col)
                k_pow = (rc - 1.0 - col) if dr == 0 else col
                qd_ref[dr, h] = jnp.exp(q_pow * lgc)
                kd_ref[dr, h] = jnp.exp(k_pow * lgc)
                cd_ref[dr, h] = jnp.exp(rc * _log_sigmoid(jnp.full((HEAD, HEAD), logit, F32)))

    for dr, (q_ref, k_ref, v_ref, o_ref) in enumerate(
            ((qf_ref, kf_ref, vf_ref, of_ref), (qb_ref, kb_ref, vb_ref, ob_ref))):
        for h in range(nh):
            sl = slice(h * HEAD, (h + 1) * HEAD)
            q = q_ref[:, sl]
            k = k_ref[:, sl]
            v = v_ref[:, sl]
            state = state_ref[dr, h]
            scores = lax.dot_general(q, k, (((1,), (1,)), ((), ())), preferred_element_type=F32)
            scores = (scores * mask_ref[dr, h]).astype(BF16)
            qs = (q.astype(F32) * qd_ref[dr, h]).astype(BF16)
            out = jnp.dot(scores, v, preferred_element_type=F32)
            out = out + jnp.dot(qs, state.astype(BF16), preferred_element_type=F32)
            o_ref[:, sl] = out
            ks = (k.astype(F32) * kd_ref[dr, h]).astype(BF16)
            upd = lax.dot_general(ks, v, (((0,), (0,)), ((), ())), preferred_element_type=F32)
            state_ref[dr, h] = state * cd_ref[dr, h] + upd


def _retention(p, dec, li, n_lat, n_ctx, d_ret):
    t = p.shape[0]
    rc = RET_BLOCK
    assert n_lat % rc == 0 and n_ctx % rc == 0
    nb, nlb, ncb = t // rc, n_lat // rc, n_ctx // rc
    nh = d_ret // HEAD
    cq, ck, cv = 2, 3, 4

    def fmap(col):
        return lambda s: (jnp.where(s < ncb, nlb + s, s - ncb), col)

    def bmap(col):
        return lambda s: (nb - 1 - s, col)

    blk = lambda im: pl.BlockSpec((rc, d_ret), im)
    return pl.pallas_call(
        functools.partial(_ret_kernel, li=li),
        out_shape=[jax.ShapeDtypeStruct((t, d_ret), F32)] * 2,
        grid=(nb,),
        in_specs=[pl.BlockSpec(memory_space=pltpu.SMEM),
                  blk(fmap(cq)), blk(fmap(ck)), blk(fmap(cv)),
                  blk(bmap(cq)), blk(bmap(ck)), blk(bmap(cv))],
        out_specs=[blk(fmap(0)), blk(bmap(0))],
        scratch_shapes=[
            pltpu.VMEM((2, nh, HEAD, HEAD), F32),
            pltpu.VMEM((2, nh, rc, rc), F32),
            pltpu.VMEM((2, nh, rc, HEAD), F32),
            pltpu.VMEM((2, nh, rc, HEAD), F32),
            pltpu.VMEM((2, nh, HEAD, HEAD), F32),
        ],
        compiler_params=_cparams("arbitrary"),
        name="retention",
    )(dec, p, p, p, p, p, p)


def _mix_kernel(u_ref, v_ref, gate_ref, of_ref, ob_ref, ws_ref, bs_ref, ya_ref, yb_ref):
    tm = u_ref.shape[0]
    ng = u_ref.shape[1] // HEAD
    for n in range(tm // SGU_CHUNK):
        rs = slice(n * SGU_CHUNK, (n + 1) * SGU_CHUNK)
        for g in range(ng):
            cs = slice(g * HEAD, (g + 1) * HEAD)
            mixed = jnp.dot(ws_ref[g], v_ref[rs, cs], preferred_element_type=F32) + bs_ref[g]
            ya_ref[rs, cs] = (u_ref[rs, cs].astype(F32) * mixed).astype(BF16)
    for h in range(of_ref.shape[1] // HEAD):
        cs = slice(h * HEAD, (h + 1) * HEAD)
        o = of_ref[:, cs] + ob_ref[:, cs]
        yb_ref[:, cs] = (_group_rms(o) * gate_ref[:, cs].astype(F32)).astype(BF16)


def _mix(p, o_f, o_b, ws, bs, li, dq):
    t = p.shape[0]
    tm = _tile(t, 256)
    row = lambda col: pl.BlockSpec((tm, dq), lambda i: (i, col))
    full = lambda a: pl.BlockSpec((None,) + a.shape[1:], lambda i: (li,) + (0,) * (a.ndim - 1))
    return pl.pallas_call(
        _mix_kernel,
        out_shape=[jax.ShapeDtypeStruct((t, dq), BF16)] * 2,
        grid=(t // tm,),
        in_specs=[row(0), row(1), row(5), row(0), row(0), full(ws), full(bs)],
        out_specs=[row(0), row(0)],
        compiler_params=_cparams("parallel"),
        name="mix",
    )(p, p, p, o_f, o_b, ws, bs)


N_BUF = 4
SM_LAG = 1
PV_LAG = 3


def _attn_kernel(q_ref, k_ref, v_ref, lam_ref, g_ref, o_ref, *scratch, tk, lam_init):
    tq = q_ref.shape[0]
    nkv = k_ref.shape[0] // tk
    s_bufs, p_bufs, a_bufs = scratch[:N_BUF], scratch[N_BUF:2 * N_BUF], scratch[2 * N_BUF:3 * N_BUF]
    acc_ref, m_ref, qs_ref = scratch[3 * N_BUF:]
    q = q_ref[...]
    lane = lax.broadcasted_iota(jnp.int32, q.shape, 1)
    zero = jnp.zeros_like(q)
    qs_ref[:tq] = jnp.where(lane < HEAD // 2, q, zero)
    qs_ref[tq:] = jnp.where(lane >= HEAD // 2, q, zero)
    acc_ref[...] = jnp.zeros_like(acc_ref)
    m_ref[...] = jnp.full(m_ref.shape, -jnp.inf, F32)

    def rows(c):
        r0 = c * tk
        return pl.ds(r0 if isinstance(r0, int) else pl.multiple_of(r0, tk), tk)

    def scores(c, par):
        s_bufs[par][...] = lax.dot_general(qs_ref[...], k_ref[rows(c), :], (((1,), (1,)), ((), ())),
                                           preferred_element_type=F32)

    def softmax(par):
        s_ref = s_bufs[par]
        m_old = m_ref[...]
        m_new = jnp.maximum(m_old, jnp.broadcast_to(s_ref[...].max(axis=1, keepdims=True), m_old.shape))
        a_bufs[par][...] = jnp.exp2(m_old - m_new)
        m_ref[...] = m_new
        for c in range(tk // HEAD):
            sl = slice(c * HEAD, (c + 1) * HEAD)
            p_bufs[par][:, sl] = jnp.exp2(s_ref[:, sl] - m_new).astype(BF16)

    def weighted_values(c, par):
        v_ext = jnp.concatenate([v_ref[rows(c), :], jnp.ones((tk, HEAD), BF16)], axis=1)
        pv = jnp.dot(p_bufs[par][...], v_ext, preferred_element_type=F32)
        alpha = a_bufs[par][...]
        acc_ref[:, :HEAD] = alpha * acc_ref[:, :HEAD] + pv[:, :HEAD]
        acc_ref[:, HEAD:] = alpha * acc_ref[:, HEAD:] + pv[:, HEAD:]

    for t in range(nkv + PV_LAG):
        if t < nkv:
            scores(t, t % N_BUF)
        if 0 <= t - SM_LAG < nkv:
            softmax((t - SM_LAG) % N_BUF)
        if 0 <= t - PV_LAG < nkv:
            weighted_values(t - PV_LAG, (t - PV_LAG) % N_BUF)

    lv = lam_ref[...]
    lam = (jnp.exp(jnp.sum(lv[0:1] * lv[1:2], axis=1, keepdims=True))
           - jnp.exp(jnp.sum(lv[2:3] * lv[3:4], axis=1, keepdims=True)) + lam_init)
    on = acc_ref[:, :HEAD] / acc_ref[:, HEAD:]
    o = on[:tq] - lam * on[tq:]
    y = o * lax.rsqrt(jnp.mean(o * o, axis=-1, keepdims=True) + EPS) * g_ref[...]
    o_ref[...] = (y * (1.0 - lam_init)).astype(BF16)


def _attention(p, lam_vecs, subln_g, li, q_row0, n_q, k_row0, n_k, d_model, lam_init):
    d_diff = d_model // 2
    nh = d_diff // HEAD
    quarter = (d_model // 4) // HEAD
    cq, ck, cv = 6 * quarter, 8 * quarter, 10 * quarter
    tq = _tile(n_q, 512)
    tk = _tile(n_k, 768)
    assert q_row0 % tq == 0 and k_row0 % n_k == 0
    qb, kb = q_row0 // tq, k_row0 // n_k
    return pl.pallas_call(
        functools.partial(_attn_kernel, tk=tk, lam_init=lam_init),
        out_shape=jax.ShapeDtypeStruct((n_q, d_diff), BF16),
        grid=(nh, n_q // tq),
        in_specs=[
            pl.BlockSpec((tq, HEAD), lambda h, i: (qb + i, cq + h)),
            pl.BlockSpec((n_k, HEAD), lambda h, i: (kb, ck + h)),
            pl.BlockSpec((n_k, HEAD), lambda h, i: (kb, cv + h)),
            pl.BlockSpec((None,) + lam_vecs.shape[1:], lambda h, i: (li, 0, 0)),
            pl.BlockSpec((None, 1, HEAD), lambda h, i: (li, 0, 0)),
        ],
        out_specs=pl.BlockSpec((tq, HEAD), lambda h, i: (i, h)),
        scratch_shapes=([pltpu.VMEM((2 * tq, tk), F32)] * N_BUF + [pltpu.VMEM((2 * tq, tk), BF16)] * N_BUF
                        + [pltpu.VMEM((2 * tq, HEAD), F32)] * N_BUF
                        + [pltpu.VMEM((2 * tq, 2 * HEAD), F32), pltpu.VMEM((2 * tq, HEAD), F32),
                           pltpu.VMEM((2 * tq, HEAD), BF16)]),
        compiler_params=_cparams("parallel", "parallel"),
        name="diffattn",
    )(p, p, p, lam_vecs, subln_g)


def _outproj_kernel(ya_ref, yb_ref, yc_ref, wa_ref, wb_ref, wc_ref, x_ref, gate_ref, o_ref):
    acc = jnp.dot(ya_ref[...], wa_ref[...], preferred_element_type=F32)
    acc = acc + jnp.dot(yb_ref[...], wb_ref[...], preferred_element_type=F32)
    acc = acc + jnp.dot(yc_ref[...], wc_ref[...], preferred_element_type=F32)
    o_ref[...] = x_ref[...] + gate_ref[...] * acc


def _outproj(ya, yb, yc, y_row0, w, x, mods, li, row):
    m, d = x.shape
    dq = d // 4
    tm = _tile(m, 512)
    assert y_row0 % tm == 0
    yb0 = y_row0 // tm
    return pl.pallas_call(
        _outproj_kernel,
        out_shape=jax.ShapeDtypeStruct((m, d), F32),
        grid=(m // tm,),
        in_specs=[
            pl.BlockSpec((tm, dq), lambda i: (yb0 + i, 0)),
            pl.BlockSpec((tm, dq), lambda i: (yb0 + i, 0)),
            pl.BlockSpec((tm, 2 * dq), lambda i: (i, 0)),
            pl.BlockSpec((None, dq, d), lambda i: (li, 0, 0)),
            pl.BlockSpec((None, dq, d), lambda i: (li, 1, 0)),
            pl.BlockSpec((None, 2 * dq, d), lambda i: (li, 1, 0)),
            pl.BlockSpec((tm, d), lambda i: (i, 0)),
            pl.BlockSpec((None, None, None, 1, d), lambda i: (li, row, 2, 0, 0)),
        ],
        out_specs=pl.BlockSpec((tm, d), lambda i: (i, 0)),
        compiler_params=_cparams("parallel"),
        name="outproj",
    )(ya, yb, yc, w, w, w, x, mods)


def _ffn_kernel(x_ref, sh_ref, sc_ref, gate_ref, g_ref, wg_ref, wu_ref, wd_ref, fg_ref, o_ref, h_ref,
                *, final_norm):
    f = pl.program_id(1)

    @pl.when(f == 0)
    def _():
        h_ref[...] = _modulated_norm(x_ref[...], g_ref[...], sh_ref[...], sc_ref[...]).astype(BF16)
        o_ref[...] = jnp.zeros_like(o_ref)

    h = h_ref[...]
    a = jnp.dot(h, wg_ref[...], preferred_element_type=F32)
    b = jnp.dot(h, wu_ref[...], preferred_element_type=F32)
    act = (a * jax.nn.sigmoid(a) * b).astype(BF16)
    o_ref[...] += jnp.dot(act, wd_ref[...], preferred_element_type=F32)

    @pl.when(f == pl.num_programs(1) - 1)
    def _():
        y = x_ref[...] + gate_ref[...] * o_ref[...]
        if final_norm:
            y = y * lax.rsqrt(jnp.mean(y * y, axis=-1, keepdims=True) + EPS) * fg_ref[...]
        o_ref[...] = y


def _ffn(x, mods, li, row, g, wg, wu, wd, final_g, final_norm):
    m, d = x.shape
    ff = wg.shape[2]
    tm = _tile(m, 1024)
    tf = _tile(ff, 256)
    mod = lambda c: pl.BlockSpec((None, None, None, 1, d), lambda i, f: (li, row, c, 0, 0))
    return pl.pallas_call(
        functools.partial(_ffn_kernel, final_norm=final_norm),
        out_shape=jax.ShapeDtypeStruct((m, d), F32),
        grid=(m // tm, ff // tf),
        in_specs=[
            pl.BlockSpec((tm, d), lambda i, f: (i, 0)),
            mod(3), mod(4), mod(5),
            pl.BlockSpec((None, 1, d), lambda i, f: (li, 0, 0)),
            pl.BlockSpec((None, d, tf), lambda i, f: (li, 0, f)),
            pl.BlockSpec((None, d, tf), lambda i, f: (li, 0, f)),
            pl.BlockSpec((None, tf, d), lambda i, f: (li, f, 0)),
            pl.BlockSpec((1, d), lambda i, f: (0, 0)),
        ],
        out_specs=pl.BlockSpec((tm, d), lambda i, f: (i, 0)),
        scratch_shapes=[pltpu.VMEM((tm, d), BF16)],
        compiler_params=_cparams("parallel", "arbitrary"),
        name="ffn",
    )(x, mods, mods, mods, g, wg, wu, wd, final_g)


def _rope_tables(n_rows, identity):
    def one(width):
        half = width // 2
        lane = jnp.arange(HEAD)
        sub = lane // width
        idx = lane % width
        first = idx < half
        freqs = ROPE_BASE ** (-(idx % half).astype(F32) / half)
        t = jnp.arange(n_rows)
        pos = jnp.where((sub % 2) == 0, (t // GRID_W)[:, None], (t % GRID_W)[:, None]).astype(F32)
        ang = pos * freqs[None, :]
        cos, sin = jnp.cos(ang), jnp.sin(ang)
        if identity:
            cos, sin = jnp.ones_like(cos), jnp.zeros_like(sin)
        return [cos, jnp.where(first[None, :], -sin, 0.0), jnp.where(first[None, :], 0.0, sin)]
    return jnp.stack(one(HEAD // 2) + one(HEAD // 4)).astype(F32)


def kernel(x, c, ctx, c_ctx, w_ada, b_ada, norm1_g, w_in, sgu_w, sgu_b, ret_decay_fwd, ret_decay_bwd,
           diff_lambda_q1, diff_lambda_k1, diff_lambda_q2, diff_lambda_k2, diff_subln_g, w_out, norm2_g,
           w_gate, w_up, w_down, final_g):
    assert x.shape[0] == 1 and ctx.shape[0] == 1
    depth = w_ada.shape[0]
    xl, xc = x[0], ctx[0]
    n_lat, d = xl.shape
    n_ctx = xc.shape[0]
    total = n_lat + n_ctx
    dq = d // 4

    mods = _ada(jnp.concatenate([c, c_ctx[None, :]], axis=0).T, w_ada, b_ada)
    mods = mods.reshape(depth, 2, N_MOD, 1, d)
    tabs_lat = _rope_tables(n_lat, identity=False)
    tabs_ctx = _rope_tables(n_ctx, identity=True)
    w_out_b = w_out.astype(BF16)
    w_gate_b, w_up_b, w_down_b = w_gate.astype(BF16), w_up.astype(BF16), w_down.astype(BF16)
    sgu_w_b = sgu_w.astype(BF16)
    sgu_b_col = sgu_b[..., None]
    g1, g2, fg = norm1_g[:, None, :], norm2_g[:, None, :], final_g[None, :]
    dec = jnp.stack([ret_decay_fwd, ret_decay_bwd]).astype(F32)
    lam_vecs = jnp.stack([diff_lambda_q1, diff_lambda_k1, diff_lambda_q2, diff_lambda_k2], axis=1).astype(F32)
    sub_g = diff_subln_g[:, None, :]

    for li in range(depth):
        need_ctx = li < depth - 1
        last = li == depth - 1
        lam_init = 0.8 - 0.6 * math.exp(-0.3 * li)
        p = _inproj(xl, mods, li, 0, g1, w_in, tabs_lat, total, 0)
        p = _inproj(xc, mods, li, 1, g1, w_in, tabs_ctx, total, n_lat, p_prev=p)
        o_f, o_b = _retention(p, dec, li, n_lat, n_ctx, dq)
        y_a, y_b = _mix(p, o_f, o_b, sgu_w_b, sgu_b_col, li, dq)
        y_c = _attention(p, lam_vecs, sub_g, li, 0, n_lat, 0, total, d, lam_init)
        xl1 = _outproj(y_a, y_b, y_c, 0, w_out_b, xl, mods, li, 0)
        xl_new = _ffn(xl1, mods, li, 0, g2, w_gate_b, w_up_b, w_down_b, fg, last)
        if need_ctx:
            yc_c = _attention(p, lam_vecs, sub_g, li, n_lat, n_ctx, n_lat, n_ctx, d, lam_init)
            xc1 = _outproj(y_a, y_b, yc_c, n_lat, w_out_b, xc, mods, li, 1)
            xc = _ffn(xc1, mods, li, 1, g2, w_gate_b, w_up_b, w_down_b, fg, False)
        xl = xl_new
    return xl[None]
```

```python
import functools
import math

import jax
import jax.numpy as jnp
import numpy as np
from jax import lax
from jax.experimental import pallas as pl
from jax.experimental.pallas import tpu as pltpu

GRID_W = 64
ROPE_BASE = 10000.0
EPS = 1e-6
N_MOD = 6
HEAD = 128
SGU_CHUNK = 128
RET_BLOCK = 256
IN_TILES = 12
ROW_CHUNK = 256
V7X_VMEM_BYTES = 64 * 1024 * 1024
VMEM_RESERVE_BYTES = 2 * 1024 * 1024

F32 = jnp.float32
BF16 = jnp.bfloat16


def _nbytes(shape, dtype):
    return math.prod(shape) * jnp.dtype(dtype).itemsize


def _cparams(sem, windows, scratch=()):
    est = 2 * sum(_nbytes(*w) for w in windows) + sum(_nbytes(*b) for b in scratch)
    limit = min(V7X_VMEM_BYTES - VMEM_RESERVE_BYTES, 2 * est + 16 * 1024 * 1024)
    return pltpu.CompilerParams(dimension_semantics=sem, vmem_limit_bytes=limit)


def _tile(n, want):
    if n <= want:
        return n
    t = want
    while n % t:
        t -= 8
    return t


def _ada_kernel(a_ref, w_ref, b_ref, o_ref, ab_ref, *, rows):
    d, tn = w_ref.shape[1], w_ref.shape[2]

    @pl.when((pl.program_id(0) == 0) & (pl.program_id(1) == 0))
    def _():
        a = a_ref[...]
        a = a * jax.nn.sigmoid(a)
        ab_ref[0] = jnp.broadcast_to(a[:, 0:1], (d, HEAD))
        ab_ref[1] = jnp.broadcast_to(a[:, 1:2], (d, HEAD))

    def body(r, acc):
        r0 = pl.multiple_of(r * rows, rows)
        a0 = ab_ref[0, pl.ds(r0, rows), :]
        a1 = ab_ref[1, pl.ds(r0, rows), :]
        new = []
        for c in range(tn // HEAD):
            w = w_ref[0, pl.ds(r0, rows), c * HEAD:(c + 1) * HEAD]
            new.append(acc[2 * c] + (w * a0).reshape(rows // 8, 8, HEAD).sum(axis=0))
            new.append(acc[2 * c + 1] + (w * a1).reshape(rows // 8, 8, HEAD).sum(axis=0))
        return tuple(new)

    z = jnp.zeros((8, HEAD), F32)
    acc = lax.fori_loop(0, d // rows, body, (z,) * (2 * (tn // HEAD)))
    for c in range(tn // HEAD):
        out = jnp.concatenate([acc[2 * c].sum(axis=0, keepdims=True), acc[2 * c + 1].sum(axis=0, keepdims=True)],
                              axis=0)
        o_ref[0, :, c * HEAD:(c + 1) * HEAD] = out + b_ref[0, :, c * HEAD:(c + 1) * HEAD]


def _ada(c2t, w_ada, b_ada):
    depth, d, n = w_ada.shape
    tn = _tile(n, 1024)
    return pl.pallas_call(
        functools.partial(_ada_kernel, rows=128),
        out_shape=jax.ShapeDtypeStruct((depth, 2, n), F32),
        grid=(depth, n // tn),
        in_specs=[
            pl.BlockSpec((d, 2), lambda l, j: (0, 0)),
            pl.BlockSpec((1, d, tn), lambda l, j: (l, 0, j)),
            pl.BlockSpec((1, 1, tn), lambda l, j: (l, 0, j)),
        ],
        out_specs=pl.BlockSpec((1, 2, tn), lambda l, j: (l, 0, j)),
        scratch_shapes=[pltpu.VMEM((2, d, HEAD), F32)],
        compiler_params=_cparams(("arbitrary", "arbitrary"), [((d, tn), F32), ((d, HEAD), F32)],
                                 [((2, d, HEAD), F32)]),
        name="ada",
    )(c2t, w_ada, b_ada.reshape(depth, 1, n))


def _rope(x, c, s1, s2, shift):
    return x * c + pltpu.roll(x, HEAD - shift, 1) * s1 + pltpu.roll(x, shift, 1) * s2


def _gelu_tanh(x):
    c = 2.0 * math.sqrt(2.0 / math.pi)
    return x * jax.nn.sigmoid(x * (c + (c * 0.044715) * (x * x)))


def _group_rms(x):
    return x * lax.rsqrt(jnp.mean(x * x, axis=-1, keepdims=True) + EPS)


def _modulated_norm(x, g, sh, sc):
    y = x * lax.rsqrt(jnp.mean(x * x, axis=-1, keepdims=True) + EPS) * g
    return y * (1.0 + sc) + sh


def _inproj_kernel(x_ref, sh_ref, sc_ref, g_ref, w_ref, tab_ref, *rest, aliased):
    o_ref, h_ref = rest[1:] if aliased else rest
    j = pl.program_id(1)
    tn = w_ref.shape[1]
    slab = min(2 * HEAD, tn)

    @pl.when(j == 0)
    def _():
        h_ref[...] = _modulated_norm(x_ref[...], g_ref[...], sh_ref[...], sc_ref[...]).astype(BF16)

    def project(fn):
        rc = min(h_ref.shape[0], ROW_CHUNK)
        for sb in range(tn // slab):
            w = w_ref[:, sb * slab:(sb + 1) * slab].astype(BF16)
            for r in range(h_ref.shape[0] // rc):
                rs = slice(r * rc, (r + 1) * rc)
                acc = jnp.dot(h_ref[rs, :], w, preferred_element_type=F32)
                for hh in range(slab // HEAD):
                    c0 = sb * slab + hh * HEAD
                    o_ref[rs, c0:c0 + HEAD] = fn(acc[:, hh * HEAD:(hh + 1) * HEAD], rs).astype(BF16)

    def rope_ret(scale):
        return lambda a, rs: _rope(a, tab_ref[0, rs, :], tab_ref[1, rs, :], tab_ref[2, rs, :], HEAD // 4) * scale

    def rope_diff(scale):
        return lambda a, rs: _rope(a, tab_ref[3, rs, :], tab_ref[4, rs, :], tab_ref[5, rs, :], HEAD // 8) * scale

    @pl.when(j == 0)
    def _():
        project(lambda a, rs: _gelu_tanh(a))

    @pl.when(j == 1)
    def _():
        project(lambda a, rs: _group_rms(_gelu_tanh(a)))

    @pl.when(j == 2)
    def _():
        project(rope_ret(1.0))

    @pl.when(j == 3)
    def _():
        project(rope_ret(HEAD ** -0.5))

    @pl.when((j == 4) | (j >= 10))
    def _():
        project(lambda a, rs: a)

    @pl.when(j == 5)
    def _():
        project(lambda a, rs: a * jax.nn.sigmoid(a))

    @pl.when((j == 6) | (j == 7))
    def _():
        project(rope_diff((HEAD // 2) ** -0.5 * math.log2(math.e)))

    @pl.when((j == 8) | (j == 9))
    def _():
        project(rope_diff(1.0))


def _inproj(x, mods, li, row, g, w, tabs, total_rows, row_off, p_prev=None):
    m, d = x.shape
    n = w.shape[2]
    tn = n // IN_TILES
    tm = _tile(m, 1024)
    assert row_off % tm == 0
    ob = row_off // tm
    aliased = p_prev is not None
    in_specs = [
        pl.BlockSpec((tm, d), lambda i, j: (i, 0)),
        pl.BlockSpec((None, None, None, 1, d), lambda i, j: (li, row, 0, 0, 0)),
        pl.BlockSpec((None, None, None, 1, d), lambda i, j: (li, row, 1, 0, 0)),
        pl.BlockSpec((None, 1, d), lambda i, j: (li, 0, 0)),
        pl.BlockSpec((None, d, tn), lambda i, j: (li, 0, j)),
        pl.BlockSpec((6, tm, HEAD), lambda i, j: (0, i, 0)),
    ]
    args = [x, mods, mods, g, w, tabs]
    if aliased:
        in_specs.append(pl.BlockSpec(memory_space=pl.ANY))
        args.append(p_prev)
    return pl.pallas_call(
        functools.partial(_inproj_kernel, aliased=aliased),
        out_shape=jax.ShapeDtypeStruct((total_rows, n), BF16),
        grid=(m // tm, IN_TILES),
        in_specs=in_specs,
        out_specs=pl.BlockSpec((tm, tn), lambda i, j: (ob + i, j)),
        scratch_shapes=[pltpu.VMEM((tm, d), BF16)],
        input_output_aliases={6: 0} if aliased else {},
        compiler_params=_cparams(("parallel", "arbitrary"),
                                 [((tm, d), x.dtype), ((d, tn), w.dtype), ((6, tm, HEAD), F32), ((tm, tn), BF16)],
                                 [((tm, d), BF16)]),
        name="inproj",
    )(*args)


def _log_sigmoid(z):
    return jnp.minimum(z, 0.0) - jnp.log1p(jnp.exp(-jnp.abs(z)))


def _ret_kernel(dec_ref, qf_ref, kf_ref, vf_ref, qb_ref, kb_ref, vb_ref, of_ref, ob_ref,
                state_ref, mask_ref, qd_ref, kd_ref, cd_ref, *, li):
    s = pl.program_id(0)
    rc = qf_ref.shape[0]
    nh = qf_ref.shape[1] // HEAD

    @pl.when(s == 0)
    def _():
        state_ref[...] = jnp.zeros_like(state_ref)
        ii = lax.broadcasted_iota(jnp.int32, (rc, rc), 0).astype(F32)
        jj = lax.broadcasted_iota(jnp.int32, (rc, rc), 1).astype(F32)
        col = lax.broadcasted_iota(jnp.int32, (rc, HEAD), 0).astype(F32)
        for dr in range(2):
            rel = (ii - jj) if dr == 0 else (jj - ii)
            for h in range(nh):
                logit = dec_ref[dr, li, h]
                lg = _log_sigmoid(jnp.full((rc, rc), logit, F32))
                mask_ref[dr, h] = jnp.where(rel >= 0, jnp.exp(jnp.maximum(rel, 0.0) * lg), 0.0)
                lgc = _log_sigmoid(jnp.full((rc, HEAD), logit, F32))
                q_pow = (col + 1.0) if dr == 0 else (rc - col)
                k_pow = (rc - 1.0 - col) if dr == 0 else col
                qd_ref[dr, h] = jnp.exp(q_pow * lgc)
                kd_ref[dr, h] = jnp.exp(k_pow * lgc)
                cd_ref[dr, h] = jnp.exp(rc * _log_sigmoid(jnp.full((HEAD, HEAD), logit, F32)))

    for dr, (q_ref, k_ref, v_ref, o_ref) in enumerate(
            ((qf_ref, kf_ref, vf_ref, of_ref), (qb_ref, kb_ref, vb_ref, ob_ref))):
        for h in range(nh):
            sl = slice(h * HEAD, (h + 1) * HEAD)
            q = q_ref[:, sl]
            k = k_ref[:, sl]
            v = v_ref[:, sl]
            state = state_ref[dr, h]
            scores = lax.dot_general(q, k, (((1,), (1,)), ((), ())), preferred_element_type=F32)
            scores = (scores * mask_ref[dr, h]).astype(BF16)
            qs = (q.astype(F32) * qd_ref[dr, h]).astype(BF16)
            out = jnp.dot(scores, v, preferred_element_type=F32)
            out = out + jnp.dot(qs, state.astype(BF16), preferred_element_type=F32)
            o_ref[:, sl] = out
            ks = (k.astype(F32) * kd_ref[dr, h]).astype(BF16)
            upd = lax.dot_general(ks, v, (((0,), (0,)), ((), ())), preferred_element_type=F32)
            state_ref[dr, h] = state * cd_ref[dr, h] + upd


def _retention(p, dec, li, n_lat, n_ctx, d_ret):
    t = p.shape[0]
    rc = RET_BLOCK
    assert n_lat % rc == 0 and n_ctx % rc == 0
    nb, nlb, ncb = t // rc, n_lat // rc, n_ctx // rc
    nh = d_ret // HEAD
    cq, ck, cv = 2, 3, 4

    def fmap(col):
        return lambda s: (jnp.where(s < ncb, nlb + s, s - ncb), col)

    def bmap(col):
        return lambda s: (nb - 1 - s, col)

    blk = lambda im: pl.BlockSpec((rc, d_ret), im)
    return pl.pallas_call(
        functools.partial(_ret_kernel, li=li),
        out_shape=[jax.ShapeDtypeStruct((t, d_ret), F32)] * 2,
        grid=(nb,),
        in_specs=[pl.BlockSpec(memory_space=pltpu.SMEM),
                  blk(fmap(cq)), blk(fmap(ck)), blk(fmap(cv)),
                  blk(bmap(cq)), blk(bmap(ck)), blk(bmap(cv))],
        out_specs=[blk(fmap(0)), blk(bmap(0))],
        scratch_shapes=[
            pltpu.VMEM((2, nh, HEAD, HEAD), F32),
            pltpu.VMEM((2, nh, rc, rc), F32),
            pltpu.VMEM((2, nh, rc, HEAD), F32),
            pltpu.VMEM((2, nh, rc, HEAD), F32),
            pltpu.VMEM((2, nh, HEAD, HEAD), F32),
        ],
        compiler_params=_cparams(("arbitrary",), [((rc, d_ret), BF16)] * 6 + [((rc, d_ret), F32)] * 2,
                                 [((2, nh, rc, rc), F32), ((2, nh, rc, HEAD), F32), ((2, nh, rc, HEAD), F32)]),
        name="retention",
    )(dec, p, p, p, p, p, p)


def _mix_kernel(u_ref, v_ref, gate_ref, of_ref, ob_ref, ws_ref, bs_ref, ya_ref, yb_ref):
    tm = u_ref.shape[0]
    ng = u_ref.shape[1] // HEAD
    for n in range(tm // SGU_CHUNK):
        rs = slice(n * SGU_CHUNK, (n + 1) * SGU_CHUNK)
        for g in range(ng):
            cs = slice(g * HEAD, (g + 1) * HEAD)
            mixed = jnp.dot(ws_ref[g], v_ref[rs, cs], preferred_element_type=F32) + bs_ref[g]
            ya_ref[rs, cs] = (u_ref[rs, cs].astype(F32) * mixed).astype(BF16)
    for h in range(of_ref.shape[1] // HEAD):
        cs = slice(h * HEAD, (h + 1) * HEAD)
        o = of_ref[:, cs] + ob_ref[:, cs]
        yb_ref[:, cs] = (_group_rms(o) * gate_ref[:, cs].astype(F32)).astype(BF16)


def _mix(p, o_f, o_b, ws, bs, li, dq):
    t = p.shape[0]
    tm = _tile(t, 256)
    row = lambda col: pl.BlockSpec((tm, dq), lambda i: (i, col))
    full = lambda a: pl.BlockSpec((None,) + a.shape[1:], lambda i: (li,) + (0,) * (a.ndim - 1))
    return pl.pallas_call(
        _mix_kernel,
        out_shape=[jax.ShapeDtypeStruct((t, dq), BF16)] * 2,
        grid=(t // tm,),
        in_specs=[row(0), row(1), row(5), row(0), row(0), full(ws), full(bs)],
        out_specs=[row(0), row(0)],
        compiler_params=_cparams(("parallel",), [((tm, dq), F32)] * 4 + [((tm, dq), BF16)] * 3),
        name="mix",
    )(p, p, p, o_f, o_b, ws, bs)


N_BUF = 4
SM_LAG = 1
PV_LAG = 3
KEY_CHUNK = 768
FIRST_KEY_CHUNK = 256


def _key_chunks(n_k):
    if n_k <= KEY_CHUNK:
        return [(0, n_k)]
    chunks, r0 = [(0, FIRST_KEY_CHUNK)], FIRST_KEY_CHUNK
    while r0 < n_k:
        size = min(KEY_CHUNK, n_k - r0)
        chunks.append((r0, size))
        r0 += size
    return chunks


def _attn_kernel(q_ref, k_ref, v_ref, lam_ref, g_ref, o_ref, *scratch, lam_init):
    tq = q_ref.shape[0]
    chunks = _key_chunks(k_ref.shape[0])
    nkv = len(chunks)
    s_bufs, p_bufs, a_bufs, pm_bufs = (scratch[i * N_BUF:(i + 1) * N_BUF] for i in range(4))
    acc_ref, m_ref, qs_ref = scratch[4 * N_BUF:]
    q = q_ref[...]
    lane = lax.broadcasted_iota(jnp.int32, q.shape, 1)
    zero = jnp.zeros_like(q)
    qs_ref[:tq] = jnp.where(lane < HEAD // 2, q, zero)
    qs_ref[tq:] = jnp.where(lane >= HEAD // 2, q, zero)
    acc_ref[...] = jnp.zeros_like(acc_ref)
    m_ref[...] = jnp.full(m_ref.shape, -jnp.inf, F32)

    def scores(c):
        r0, size = chunks[c]
        s = lax.dot_general(qs_ref[...], k_ref[r0:r0 + size, :], (((1,), (1,)), ((), ())),
                            preferred_element_type=F32)
        s_bufs[c % N_BUF][:, :size] = s
        pm = s[:, :HEAD]
        for j in range(1, size // HEAD):
            pm = jnp.maximum(pm, s[:, j * HEAD:(j + 1) * HEAD])
        pm_bufs[c % N_BUF][...] = pm

    def softmax(c):
        size = chunks[c][1]
        s_ref, p_ref = s_bufs[c % N_BUF], p_bufs[c % N_BUF]
        m_old = m_ref[...]
        m_new = jnp.maximum(m_old, jnp.broadcast_to(pm_bufs[c % N_BUF][...].max(axis=1, keepdims=True),
                                                    m_old.shape))
        a_bufs[c % N_BUF][...] = jnp.exp2(m_old - m_new)
        m_ref[...] = m_new
        for j in range(size // HEAD):
            sl = slice(j * HEAD, (j + 1) * HEAD)
            p_ref[:, sl] = jnp.exp2(s_ref[:, sl] - m_new).astype(BF16)

    def weighted_values(c):
        r0, size = chunks[c]
        v_ext = jnp.concatenate([v_ref[r0:r0 + size, :], jnp.ones((size, HEAD), BF16)], axis=1)
        pv = jnp.dot(p_bufs[c % N_BUF][:, :size], v_ext, preferred_element_type=F32)
        alpha = a_bufs[c % N_BUF][...]
        acc_ref[:, :HEAD] = alpha * acc_ref[:, :HEAD] + pv[:, :HEAD]
        acc_ref[:, HEAD:] = alpha * acc_ref[:, HEAD:] + pv[:, HEAD:]

    for t in range(nkv + PV_LAG):
        if t < nkv:
            scores(t)
        if 0 <= t - SM_LAG < nkv:
            softmax(t - SM_LAG)
        if 0 <= t - PV_LAG < nkv:
            weighted_values(t - PV_LAG)

    lv = lam_ref[...]
    lam = (jnp.exp(jnp.sum(lv[0:1] * lv[1:2], axis=1, keepdims=True))
           - jnp.exp(jnp.sum(lv[2:3] * lv[3:4], axis=1, keepdims=True)) + lam_init)
    on = acc_ref[:, :HEAD] / acc_ref[:, HEAD:]
    o = on[:tq] - lam * on[tq:]
    y = o * lax.rsqrt(jnp.mean(o * o, axis=-1, keepdims=True) + EPS) * g_ref[...]
    o_ref[...] = (y * (1.0 - lam_init)).astype(BF16)


def _attention(p, lam_vecs, subln_g, li, q_row0, n_q, k_row0, n_k, d_model, lam_init):
    d_diff = d_model // 2
    nh = d_diff // HEAD
    quarter = (d_model // 4) // HEAD
    cq, ck, cv = 6 * quarter, 8 * quarter, 10 * quarter
    tq = _tile(n_q, 512)
    tk = max(size for _, size in _key_chunks(n_k))
    assert n_k % HEAD == 0
    assert q_row0 % tq == 0 and k_row0 % n_k == 0
    qb, kb = q_row0 // tq, k_row0 // n_k
    return pl.pallas_call(
        functools.partial(_attn_kernel, lam_init=lam_init),
        out_shape=jax.ShapeDtypeStruct((n_q, d_diff), BF16),
        grid=(nh, n_q // tq),
        in_specs=[
            pl.BlockSpec((tq, HEAD), lambda h, i: (qb + i, cq + h)),
            pl.BlockSpec((n_k, HEAD), lambda h, i: (kb, ck + h)),
            pl.BlockSpec((n_k, HEAD), lambda h, i: (kb, cv + h)),
            pl.BlockSpec((None,) + lam_vecs.shape[1:], lambda h, i: (li, 0, 0)),
            pl.BlockSpec((None, 1, HEAD), lambda h, i: (li, 0, 0)),
        ],
        out_specs=pl.BlockSpec((tq, HEAD), lambda h, i: (i, h)),
        scratch_shapes=([pltpu.VMEM((2 * tq, tk), F32)] * N_BUF + [pltpu.VMEM((2 * tq, tk), BF16)] * N_BUF
                        + [pltpu.VMEM((2 * tq, HEAD), F32)] * (2 * N_BUF)
                        + [pltpu.VMEM((2 * tq, 2 * HEAD), F32), pltpu.VMEM((2 * tq, HEAD), F32),
                           pltpu.VMEM((2 * tq, HEAD), BF16)]),
        compiler_params=_cparams(("parallel", "parallel"), [((n_k, HEAD), BF16)] * 2 + [((tq, HEAD), BF16)] * 2,
                                 [((2 * tq, tk), F32)] * N_BUF + [((2 * tq, tk), BF16)] * N_BUF
                                 + [((2 * tq, HEAD), F32)] * (2 * N_BUF + 4)),
        name="diffattn",
    )(p, p, p, lam_vecs, subln_g)


def _outproj_kernel(ya_ref, yb_ref, yc_ref, wa_ref, wb_ref, wc_ref, x_ref, gate_ref, o_ref):
    acc = jnp.dot(ya_ref[...], wa_ref[...], preferred_element_type=F32)
    acc = acc + jnp.dot(yb_ref[...], wb_ref[...], preferred_element_type=F32)
    acc = acc + jnp.dot(yc_ref[...], wc_ref[...], preferred_element_type=F32)
    o_ref[...] = x_ref[...] + gate_ref[...] * acc


def _outproj(ya, yb, yc, y_row0, w, x, mods, li, row):
    m, d = x.shape
    dq = d // 4
    tm = _tile(m, 512)
    assert y_row0 % tm == 0
    yb0 = y_row0 // tm
    return pl.pallas_call(
        _outproj_kernel,
        out_shape=jax.ShapeDtypeStruct((m, d), F32),
        grid=(m // tm,),
        in_specs=[
            pl.BlockSpec((tm, dq), lambda i: (yb0 + i, 0)),
            pl.BlockSpec((tm, dq), lambda i: (yb0 + i, 0)),
            pl.BlockSpec((tm, 2 * dq), lambda i: (i, 0)),
            pl.BlockSpec((None, dq, d), lambda i: (li, 0, 0)),
            pl.BlockSpec((None, dq, d), lambda i: (li, 1, 0)),
            pl.BlockSpec((None, 2 * dq, d), lambda i: (li, 1, 0)),
            pl.BlockSpec((tm, d), lambda i: (i, 0)),
            pl.BlockSpec((None, None, None, 1, d), lambda i: (li, row, 2, 0, 0)),
        ],
        out_specs=pl.BlockSpec((tm, d), lambda i: (i, 0)),
        compiler_params=_cparams(("parallel",), [((tm, d), BF16), ((d, d), BF16), ((tm, d), F32), ((tm, d), F32)]),
        name="outproj",
    )(ya, yb, yc, w, w, w, x, mods)


def _ffn_kernel(x_ref, sh_ref, sc_ref, gate_ref, g_ref, wg_ref, wu_ref, wd_ref, fg_ref, o_ref, h_ref,
                *, final_norm):
    f = pl.program_id(1)

    @pl.when(f == 0)
    def _():
        h_ref[...] = _modulated_norm(x_ref[...], g_ref[...], sh_ref[...], sc_ref[...]).astype(BF16)
        o_ref[...] = jnp.zeros_like(o_ref)

    h = h_ref[...]
    a = jnp.dot(h, wg_ref[...].astype(BF16), preferred_element_type=F32)
    b = jnp.dot(h, wu_ref[...].astype(BF16), preferred_element_type=F32)
    act = (a * jax.nn.sigmoid(a) * b).astype(BF16)
    o_ref[...] += jnp.dot(act, wd_ref[...].astype(BF16), preferred_element_type=F32)

    @pl.when(f == pl.num_programs(1) - 1)
    def _():
        y = x_ref[...] + gate_ref[...] * o_ref[...]
        if final_norm:
            y = y * lax.rsqrt(jnp.mean(y * y, axis=-1, keepdims=True) + EPS) * fg_ref[...]
        o_ref[...] = y


def _ffn(x, mods, li, row, g, wg, wu, wd, final_g, final_norm):
    m, d = x.shape
    ff = wg.shape[2]
    tm = _tile(m, 1024)
    tf = _tile(ff, 256)
    mod = lambda c: pl.BlockSpec((None, None, None, 1, d), lambda i, f: (li, row, c, 0, 0))
    return pl.pallas_call(
        functools.partial(_ffn_kernel, final_norm=final_norm),
        out_shape=jax.ShapeDtypeStruct((m, d), F32),
        grid=(m // tm, ff // tf),
        in_specs=[
            pl.BlockSpec((tm, d), lambda i, f: (i, 0), pipeline_mode=pl.Buffered(1)),
            mod(3), mod(4), mod(5),
            pl.BlockSpec((None, 1, d), lambda i, f: (li, 0, 0)),
            pl.BlockSpec((None, d, tf), lambda i, f: (li, 0, f)),
            pl.BlockSpec((None, d, tf), lambda i, f: (li, 0, f)),
            pl.BlockSpec((None, tf, d), lambda i, f: (li, f, 0)),
            pl.BlockSpec((1, d), lambda i, f: (0, 0)),
        ],
        out_specs=pl.BlockSpec((tm, d), lambda i, f: (i, 0)),
        scratch_shapes=[pltpu.VMEM((tm, d), BF16)],
        compiler_params=_cparams(("parallel", "arbitrary"),
                                 [((tm, d), F32), ((tm, d), F32)] + [((d, tf), wg.dtype)] * 3, [((tm, d), BF16)]),
        name="ffn",
    )(x, mods, mods, mods, g, wg, wu, wd, final_g)


def _rope_tables(n_rows, identity):
    def one(width):
        half = width // 2
        lane = np.arange(HEAD)
        sub = lane // width
        idx = lane % width
        first = idx < half
        freqs = np.float32(ROPE_BASE) ** (-(idx % half).astype(np.float32) / np.float32(half))
        t = np.arange(n_rows)
        pos = np.where((sub % 2) == 0, (t // GRID_W)[:, None], (t % GRID_W)[:, None]).astype(np.float32)
        ang = pos * freqs[None, :]
        cos, sin = np.cos(ang), np.sin(ang)
        if identity:
            cos, sin = np.ones_like(cos), np.zeros_like(sin)
        return [cos, np.where(first[None, :], -sin, 0.0), np.where(first[None, :], 0.0, sin)]
    return jnp.asarray(np.stack(one(HEAD // 2) + one(HEAD // 4)).astype(np.float32))


def kernel(x, c, ctx, c_ctx, w_ada, b_ada, norm1_g, w_in, sgu_w, sgu_b, ret_decay_fwd, ret_decay_bwd,
           diff_lambda_q1, diff_lambda_k1, diff_lambda_q2, diff_lambda_k2, diff_subln_g, w_out, norm2_g,
           w_gate, w_up, w_down, final_g):
    assert x.shape[0] == 1 and ctx.shape[0] == 1
    depth = w_ada.shape[0]
    xl, xc = x[0], ctx[0]
    n_lat, d = xl.shape
    n_ctx = xc.shape[0]
    total = n_lat + n_ctx
    dq = d // 4

    mods = _ada(jnp.concatenate([c, c_ctx[None, :]], axis=0).T, w_ada, b_ada)
    mods = mods.reshape(depth, 2, N_MOD, 1, d)
    tabs_lat = _rope_tables(n_lat, identity=False)
    tabs_ctx = _rope_tables(n_ctx, identity=True)
    w_out_b = w_out.astype(BF16)
    sgu_w_b = sgu_w.astype(BF16)
    sgu_b_col = sgu_b[..., None]
    g1, g2, fg = norm1_g[:, None, :], norm2_g[:, None, :], final_g[None, :]
    dec = jnp.stack([ret_decay_fwd, ret_decay_bwd]).astype(F32)
    lam_vecs = jnp.stack([diff_lambda_q1, diff_lambda_k1, diff_lambda_q2, diff_lambda_k2], axis=1).astype(F32)
    sub_g = diff_subln_g[:, None, :]

    for li in range(depth):
        need_ctx = li < depth - 1
        last = li == depth - 1
        lam_init = 0.8 - 0.6 * math.exp(-0.3 * li)
        p = _inproj(xl, mods, li, 0, g1, w_in, tabs_lat, total, 0)
        p = _inproj(xc, mods, li, 1, g1, w_in, tabs_ctx, total, n_lat, p_prev=p)
        o_f, o_b = _retention(p, dec, li, n_lat, n_ctx, dq)
        y_a, y_b = _mix(p, o_f, o_b, sgu_w_b, sgu_b_col, li, dq)
        y_c = _attention(p, lam_vecs, sub_g, li, 0, n_lat, 0, total, d, lam_init)
        xl1 = _outproj(y_a, y_b, y_c, 0, w_out_b, xl, mods, li, 0)
        xl_new = _ffn(xl1, mods, li, 0, g2, w_gate, w_up, w_down, fg, last)
        if need_ctx:
            yc_c = _attention(p, lam_vecs, sub_g, li, n_lat, n_ctx, n_lat, n_ctx, d, lam_init)
            xc1 = _outproj(y_a, y_b, yc_c, n_lat, w_out_b, xc, mods, li, 1)
            xc = _ffn(xc1, mods, li, 1, g2, w_gate, w_up, w_down, fg, False)
        xl = xl_new
    return xl[None]
```

```python
import functools
import math

import jax
import jax.numpy as jnp
import numpy as np
from jax import lax
from jax.experimental import pallas as pl
from jax.experimental.pallas import tpu as pltpu

GRID_W = 64
ROPE_BASE = 10000.0
EPS = 1e-6
N_MOD = 6
HEAD = 128
SGU_CHUNK = 128
RET_BLOCK = 256
IN_TILES = 6
ROW_CHUNK = 256
V7X_VMEM_BYTES = 64 * 1024 * 1024
VMEM_RESERVE_BYTES = 2 * 1024 * 1024

F32 = jnp.float32
BF16 = jnp.bfloat16


def _nbytes(shape, dtype):
    return math.prod(shape) * jnp.dtype(dtype).itemsize


def _cparams(sem, windows, scratch=()):
    est = 2 * sum(_nbytes(*w) for w in windows) + sum(_nbytes(*b) for b in scratch)
    limit = min(V7X_VMEM_BYTES - VMEM_RESERVE_BYTES, 2 * est + 16 * 1024 * 1024)
    return pltpu.CompilerParams(dimension_semantics=sem, vmem_limit_bytes=limit)


def _tile(n, want):
    if n <= want:
        return n
    t = want
    while n % t:
        t -= 8
    return t


def _ada_kernel(a_ref, w_ref, b_ref, o_ref, ab_ref, *, rows):
    d, tn = w_ref.shape[1], w_ref.shape[2]

    @pl.when((pl.program_id(0) == 0) & (pl.program_id(1) == 0))
    def _():
        a = a_ref[...]
        a = a * jax.nn.sigmoid(a)
        ab_ref[0] = jnp.broadcast_to(a[:, 0:1], (d, HEAD))
        ab_ref[1] = jnp.broadcast_to(a[:, 1:2], (d, HEAD))

    def body(r, acc):
        r0 = pl.multiple_of(r * rows, rows)
        a0 = ab_ref[0, pl.ds(r0, rows), :]
        a1 = ab_ref[1, pl.ds(r0, rows), :]
        new = []
        for c in range(tn // HEAD):
            w = w_ref[0, pl.ds(r0, rows), c * HEAD:(c + 1) * HEAD]
            new.append(acc[2 * c] + (w * a0).reshape(rows // 8, 8, HEAD).sum(axis=0))
            new.append(acc[2 * c + 1] + (w * a1).reshape(rows // 8, 8, HEAD).sum(axis=0))
        return tuple(new)

    z = jnp.zeros((8, HEAD), F32)
    acc = lax.fori_loop(0, d // rows, body, (z,) * (2 * (tn // HEAD)))
    for c in range(tn // HEAD):
        out = jnp.concatenate([acc[2 * c].sum(axis=0, keepdims=True), acc[2 * c + 1].sum(axis=0, keepdims=True)],
                              axis=0)
        o_ref[0, :, c * HEAD:(c + 1) * HEAD] = out + b_ref[0, :, c * HEAD:(c + 1) * HEAD]


def _ada(c2t, w_ada, b_ada):
    depth, d, n = w_ada.shape
    tn = _tile(n, 1024)
    return pl.pallas_call(
        functools.partial(_ada_kernel, rows=128),
        out_shape=jax.ShapeDtypeStruct((depth, 2, n), F32),
        grid=(depth, n // tn),
        in_specs=[
            pl.BlockSpec((d, 2), lambda l, j: (0, 0)),
            pl.BlockSpec((1, d, tn), lambda l, j: (l, 0, j)),
            pl.BlockSpec((1, 1, tn), lambda l, j: (l, 0, j)),
        ],
        out_specs=pl.BlockSpec((1, 2, tn), lambda l, j: (l, 0, j)),
        scratch_shapes=[pltpu.VMEM((2, d, HEAD), F32)],
        compiler_params=_cparams(("arbitrary", "arbitrary"), [((d, tn), F32), ((d, HEAD), F32)],
                                 [((2, d, HEAD), F32)]),
        name="ada",
    )(c2t, w_ada, b_ada.reshape(depth, 1, n))


def _rope(x, c, s1, s2, shift):
    return x * c + pltpu.roll(x, HEAD - shift, 1) * s1 + pltpu.roll(x, shift, 1) * s2


def _gelu_tanh(x):
    c = 2.0 * math.sqrt(2.0 / math.pi)
    return x * jax.nn.sigmoid(x * (c + (c * 0.044715) * (x * x)))


def _group_rms(x):
    return x * lax.rsqrt(jnp.mean(x * x, axis=-1, keepdims=True) + EPS)


def _modulated_norm(x, g, sh, sc):
    y = x * lax.rsqrt(jnp.mean(x * x, axis=-1, keepdims=True) + EPS) * g
    return y * (1.0 + sc) + sh


def _inproj_kernel(x_ref, sh_ref, sc_ref, g_ref, w_ref, tab_ref, *rest, aliased):
    o_ref, h_ref = rest[1:] if aliased else rest
    j = pl.program_id(1)
    tn = w_ref.shape[1]

    @pl.when(j == 0)
    def _():
        h_ref[...] = _modulated_norm(x_ref[...], g_ref[...], sh_ref[...], sc_ref[...]).astype(BF16)

    def project(*fns):
        rc = min(h_ref.shape[0], ROW_CHUNK)
        part = tn // len(fns)
        slab = min(2 * HEAD, part)
        for c0 in range(0, tn, slab):
            fn = fns[c0 // part]
            w = w_ref[:, c0:c0 + slab].astype(BF16)
            for r in range(h_ref.shape[0] // rc):
                rs = slice(r * rc, (r + 1) * rc)
                acc = jnp.dot(h_ref[rs, :], w, preferred_element_type=F32)
                for hh in range(slab // HEAD):
                    o_ref[rs, c0 + hh * HEAD:c0 + (hh + 1) * HEAD] = fn(
                        acc[:, hh * HEAD:(hh + 1) * HEAD], rs).astype(BF16)

    def rope_ret(scale):
        return lambda a, rs: _rope(a, tab_ref[0, rs, :], tab_ref[1, rs, :], tab_ref[2, rs, :], HEAD // 4) * scale

    def rope_diff(scale):
        return lambda a, rs: _rope(a, tab_ref[3, rs, :], tab_ref[4, rs, :], tab_ref[5, rs, :], HEAD // 8) * scale

    @pl.when(j == 0)
    def _():
        project(lambda a, rs: _gelu_tanh(a), lambda a, rs: _group_rms(_gelu_tanh(a)))

    @pl.when(j == 1)
    def _():
        project(rope_ret(1.0), rope_ret(HEAD ** -0.5))

    @pl.when(j == 2)
    def _():
        project(lambda a, rs: a, lambda a, rs: a * jax.nn.sigmoid(a))

    @pl.when(j == 3)
    def _():
        project(rope_diff((HEAD // 2) ** -0.5 * math.log2(math.e)))

    @pl.when(j == 4)
    def _():
        project(rope_diff(1.0))

    @pl.when(j == 5)
    def _():
        project(lambda a, rs: a)


def _inproj(x, mods, li, row, g, w, tabs, total_rows, row_off, p_prev=None):
    m, d = x.shape
    n = w.shape[2]
    tn = n // IN_TILES
    tm = _tile(m, 1024)
    assert row_off % tm == 0
    ob = row_off // tm
    aliased = p_prev is not None
    in_specs = [
        pl.BlockSpec((tm, d), lambda i, j: (i, 0)),
        pl.BlockSpec((None, None, None, 1, d), lambda i, j: (li, row, 0, 0, 0)),
        pl.BlockSpec((None, None, None, 1, d), lambda i, j: (li, row, 1, 0, 0)),
        pl.BlockSpec((None, 1, d), lambda i, j: (li, 0, 0)),
        pl.BlockSpec((None, d, tn), lambda i, j: (li, 0, j)),
        pl.BlockSpec((6, tm, HEAD), lambda i, j: (0, i, 0)),
    ]
    args = [x, mods, mods, g, w, tabs]
    if aliased:
        in_specs.append(pl.BlockSpec(memory_space=pl.ANY))
        args.append(p_prev)
    return pl.pallas_call(
        functools.partial(_inproj_kernel, aliased=aliased),
        out_shape=jax.ShapeDtypeStruct((total_rows, n), BF16),
        grid=(m // tm, IN_TILES),
        in_specs=in_specs,
        out_specs=pl.BlockSpec((tm, tn), lambda i, j: (ob + i, j)),
        scratch_shapes=[pltpu.VMEM((tm, d), BF16)],
        input_output_aliases={6: 0} if aliased else {},
        compiler_params=_cparams(("parallel", "arbitrary"),
                                 [((tm, d), x.dtype), ((d, tn), w.dtype), ((6, tm, HEAD), F32), ((tm, tn), BF16)],
                                 [((tm, d), BF16)]),
        name="inproj",
    )(*args)


def _log_sigmoid(z):
    return jnp.minimum(z, 0.0) - jnp.log1p(jnp.exp(-jnp.abs(z)))


def _ret_kernel(dec_ref, qf_ref, kf_ref, vf_ref, qb_ref, kb_ref, vb_ref, of_ref, ob_ref,
                state_ref, mask_ref, qd_ref, kd_ref, cd_ref, *, li):
    s = pl.program_id(0)
    rc = qf_ref.shape[0]
    nh = qf_ref.shape[1] // HEAD

    @pl.when(s == 0)
    def _():
        state_ref[...] = jnp.zeros_like(state_ref)
        ii = lax.broadcasted_iota(jnp.int32, (rc, rc), 0).astype(F32)
        jj = lax.broadcasted_iota(jnp.int32, (rc, rc), 1).astype(F32)
        col = lax.broadcasted_iota(jnp.int32, (rc, HEAD), 0).astype(F32)
        for dr in range(2):
            rel = (ii - jj) if dr == 0 else (jj - ii)
            for h in range(nh):
                logit = dec_ref[dr, li, h]
                lg = _log_sigmoid(jnp.full((rc, rc), logit, F32))
                mask_ref[dr, h] = jnp.where(rel >= 0, jnp.exp(jnp.maximum(rel, 0.0) * lg), 0.0)
                lgc = _log_sigmoid(jnp.full((rc, HEAD), logit, F32))
                q_pow = (col + 1.0) if dr == 0 else (rc - col)
                k_pow = (rc - 1.0 - col) if dr == 0 else col
                qd_ref[dr, h] = jnp.exp(q_pow * lgc)
                kd_ref[dr, h] = jnp.exp(k_pow * lgc)
                cd_ref[dr, h] = jnp.exp(rc * _log_sigmoid(jnp.full((HEAD, HEAD), logit, F32)))

    for dr, (q_ref, k_ref, v_ref, o_ref) in enumerate(
            ((qf_ref, kf_ref, vf_ref, of_ref), (qb_ref, kb_ref, vb_ref, ob_ref))):
        for h in range(nh):
            sl = slice(h * HEAD, (h + 1) * HEAD)
            q = q_ref[:, sl]
            k = k_ref[:, sl]
            v = v_ref[:, sl]
            state = state_ref[dr, h]
            scores = lax.dot_general(q, k, (((1,), (1,)), ((), ())), preferred_element_type=F32)
            scores = (scores * mask_ref[dr, h]).astype(BF16)
            qs = (q.astype(F32) * qd_ref[dr, h]).astype(BF16)
            out = jnp.dot(scores, v, preferred_element_type=F32)
            out = out + jnp.dot(qs, state.astype(BF16), preferred_element_type=F32)
            o_ref[:, sl] = out
            ks = (k.astype(F32) * kd_ref[dr, h]).astype(BF16)
            upd = lax.dot_general(ks, v, (((0,), (0,)), ((), ())), preferred_element_type=F32)
            state_ref[dr, h] = state * cd_ref[dr, h] + upd


def _retention(p, dec, li, n_lat, n_ctx, d_ret):
    t = p.shape[0]
    rc = RET_BLOCK
    assert n_lat % rc == 0 and n_ctx % rc == 0
    nb, nlb, ncb = t // rc, n_lat // rc, n_ctx // rc
    nh = d_ret // HEAD
    cq, ck, cv = 2, 3, 4

    def fmap(col):
        return lambda s: (jnp.where(s < ncb, nlb + s, s - ncb), col)

    def bmap(col):
        return lambda s: (nb - 1 - s, col)

    blk = lambda im: pl.BlockSpec((rc, d_ret), im)
    return pl.pallas_call(
        functools.partial(_ret_kernel, li=li),
        out_shape=[jax.ShapeDtypeStruct((t, d_ret), F32)] * 2,
        grid=(nb,),
        in_specs=[pl.BlockSpec(memory_space=pltpu.SMEM),
                  blk(fmap(cq)), blk(fmap(ck)), blk(fmap(cv)),
                  blk(bmap(cq)), blk(bmap(ck)), blk(bmap(cv))],
        out_specs=[blk(fmap(0)), blk(bmap(0))],
        scratch_shapes=[
            pltpu.VMEM((2, nh, HEAD, HEAD), F32),
            pltpu.VMEM((2, nh, rc, rc), F32),
            pltpu.VMEM((2, nh, rc, HEAD), F32),
            pltpu.VMEM((2, nh, rc, HEAD), F32),
            pltpu.VMEM((2, nh, HEAD, HEAD), F32),
        ],
        compiler_params=_cparams(("arbitrary",), [((rc, d_ret), BF16)] * 6 + [((rc, d_ret), F32)] * 2,
                                 [((2, nh, rc, rc), F32), ((2, nh, rc, HEAD), F32), ((2, nh, rc, HEAD), F32)]),
        name="retention",
    )(dec, p, p, p, p, p, p)


N_BUF = 4
SM_LAG = 1
PV_LAG = 3
KEY_CHUNK = 768
FIRST_KEY_CHUNK = 256


def _key_chunks(n_k):
    if n_k <= KEY_CHUNK:
        return [(0, n_k)]
    chunks, r0 = [(0, FIRST_KEY_CHUNK)], FIRST_KEY_CHUNK
    while r0 < n_k:
        size = min(KEY_CHUNK, n_k - r0)
        chunks.append((r0, size))
        r0 += size
    return chunks


def _attn_kernel(q_ref, k_ref, v_ref, lam_ref, g_ref, o_ref, *scratch, lam_init):
    tq = q_ref.shape[0]
    chunks = _key_chunks(k_ref.shape[0])
    nkv = len(chunks)
    s_bufs, p_bufs, a_bufs, pm_bufs = (scratch[i * N_BUF:(i + 1) * N_BUF] for i in range(4))
    acc_ref, m_ref, qs_ref = scratch[4 * N_BUF:]
    q = q_ref[...]
    lane = lax.broadcasted_iota(jnp.int32, q.shape, 1)
    zero = jnp.zeros_like(q)
    qs_ref[:tq] = jnp.where(lane < HEAD // 2, q, zero)
    qs_ref[tq:] = jnp.where(lane >= HEAD // 2, q, zero)
    acc_ref[...] = jnp.zeros_like(acc_ref)
    m_ref[...] = jnp.full(m_ref.shape, -jnp.inf, F32)

    def scores(c):
        r0, size = chunks[c]
        s = lax.dot_general(qs_ref[...], k_ref[r0:r0 + size, :], (((1,), (1,)), ((), ())),
                            preferred_element_type=F32)
        s_bufs[c % N_BUF][:, :size] = s
        pm = s[:, :HEAD]
        for j in range(1, size // HEAD):
            pm = jnp.maximum(pm, s[:, j * HEAD:(j + 1) * HEAD])
        pm_bufs[c % N_BUF][...] = pm

    def softmax(c):
        size = chunks[c][1]
        s_ref, p_ref = s_bufs[c % N_BUF], p_bufs[c % N_BUF]
        m_old = m_ref[...]
        m_new = jnp.maximum(m_old, jnp.broadcast_to(pm_bufs[c % N_BUF][...].max(axis=1, keepdims=True),
                                                    m_old.shape))
        a_bufs[c % N_BUF][...] = jnp.exp2(m_old - m_new)
        m_ref[...] = m_new
        for j in range(size // HEAD):
            sl = slice(j * HEAD, (j + 1) * HEAD)
            p_ref[:, sl] = jnp.exp2(s_ref[:, sl] - m_new).astype(BF16)

    def weighted_values(c):
        r0, size = chunks[c]
        v_ext = jnp.concatenate([v_ref[r0:r0 + size, :], jnp.ones((size, HEAD), BF16)], axis=1)
        pv = jnp.dot(p_bufs[c % N_BUF][:, :size], v_ext, preferred_element_type=F32)
        alpha = a_bufs[c % N_BUF][...]
        acc_ref[:, :HEAD] = alpha * acc_ref[:, :HEAD] + pv[:, :HEAD]
        acc_ref[:, HEAD:] = alpha * acc_ref[:, HEAD:] + pv[:, HEAD:]

    for t in range(nkv + PV_LAG):
        if t < nkv:
            scores(t)
        if 0 <= t - SM_LAG < nkv:
            softmax(t - SM_LAG)
        if 0 <= t - PV_LAG < nkv:
            weighted_values(t - PV_LAG)

    lv = lam_ref[...]
    lam = (jnp.exp(jnp.sum(lv[0:1] * lv[1:2], axis=1, keepdims=True))
           - jnp.exp(jnp.sum(lv[2:3] * lv[3:4], axis=1, keepdims=True)) + lam_init)
    on = acc_ref[:, :HEAD] / acc_ref[:, HEAD:]
    o = on[:tq] - lam * on[tq:]
    y = o * lax.rsqrt(jnp.mean(o * o, axis=-1, keepdims=True) + EPS) * g_ref[...]
    o_ref[...] = (y * (1.0 - lam_init)).astype(BF16)


def _attention(p, lam_vecs, subln_g, li, q_row0, n_q, k_row0, n_k, d_model, lam_init):
    d_diff = d_model // 2
    nh = d_diff // HEAD
    quarter = (d_model // 4) // HEAD
    cq, ck, cv = 6 * quarter, 8 * quarter, 10 * quarter
    tq = _tile(n_q, 512)
    tk = max(size for _, size in _key_chunks(n_k))
    assert n_k % HEAD == 0
    assert q_row0 % tq == 0 and k_row0 % n_k == 0
    qb, kb = q_row0 // tq, k_row0 // n_k
    return pl.pallas_call(
        functools.partial(_attn_kernel, lam_init=lam_init),
        out_shape=jax.ShapeDtypeStruct((n_q, d_diff), BF16),
        grid=(nh, n_q // tq),
        in_specs=[
            pl.BlockSpec((tq, HEAD), lambda h, i: (qb + i, cq + h)),
            pl.BlockSpec((n_k, HEAD), lambda h, i: (kb, ck + h)),
            pl.BlockSpec((n_k, HEAD), lambda h, i: (kb, cv + h)),
            pl.BlockSpec((None,) + lam_vecs.shape[1:], lambda h, i: (li, 0, 0)),
            pl.BlockSpec((None, 1, HEAD), lambda h, i: (li, 0, 0)),
        ],
        out_specs=pl.BlockSpec((tq, HEAD), lambda h, i: (i, h)),
        scratch_shapes=([pltpu.VMEM((2 * tq, tk), F32)] * N_BUF + [pltpu.VMEM((2 * tq, tk), BF16)] * N_BUF
                        + [pltpu.VMEM((2 * tq, HEAD), F32)] * (2 * N_BUF)
                        + [pltpu.VMEM((2 * tq, 2 * HEAD), F32), pltpu.VMEM((2 * tq, HEAD), F32),
                           pltpu.VMEM((2 * tq, HEAD), BF16)]),
        compiler_params=_cparams(("parallel", "parallel"), [((n_k, HEAD), BF16)] * 2 + [((tq, HEAD), BF16)] * 2,
                                 [((2 * tq, tk), F32)] * N_BUF + [((2 * tq, tk), BF16)] * N_BUF
                                 + [((2 * tq, HEAD), F32)] * (2 * N_BUF + 4)),
        name="diffattn",
    )(p, p, p, lam_vecs, subln_g)


def _outproj_kernel(u_ref, v_ref, gate_ref, of_ref, ob_ref, ws_ref, bs_ref, yc_ref, wa_ref, wb_ref, wc_ref,
                    x_ref, g1_ref, o_ref, ya_ref, yb_ref):
    tm = x_ref.shape[0]
    for n in range(tm // SGU_CHUNK):
        rs = slice(n * SGU_CHUNK, (n + 1) * SGU_CHUNK)
        for g in range(u_ref.shape[1] // HEAD):
            cs = slice(g * HEAD, (g + 1) * HEAD)
            mixed = jnp.dot(ws_ref[g], v_ref[rs, cs], preferred_element_type=F32) + bs_ref[g]
            ya_ref[rs, cs] = (u_ref[rs, cs].astype(F32) * mixed).astype(BF16)
    for h in range(of_ref.shape[1] // HEAD):
        cs = slice(h * HEAD, (h + 1) * HEAD)
        o = of_ref[:, cs] + ob_ref[:, cs]
        yb_ref[:, cs] = (_group_rms(o) * gate_ref[:, cs].astype(F32)).astype(BF16)
    acc = jnp.dot(ya_ref[...], wa_ref[...], preferred_element_type=F32)
    acc = acc + jnp.dot(yb_ref[...], wb_ref[...], preferred_element_type=F32)
    acc = acc + jnp.dot(yc_ref[...], wc_ref[...], preferred_element_type=F32)
    o_ref[...] = x_ref[...] + g1_ref[...] * acc


def _outproj(p, o_f, o_b, ws, bs, yc, y_row0, w, x, mods, li, row):
    m, d = x.shape
    dq = d // 4
    tm = _tile(m, 512)
    assert y_row0 % tm == 0 and tm % SGU_CHUNK == 0
    yb0 = y_row0 // tm
    tok = lambda col: pl.BlockSpec((tm, dq), lambda i: (yb0 + i, col))
    layer = lambda a: pl.BlockSpec((None,) + a.shape[1:], lambda i: (li,) + (0,) * (a.ndim - 1))
    return pl.pallas_call(
        _outproj_kernel,
        out_shape=jax.ShapeDtypeStruct((m, d), F32),
        grid=(m // tm,),
        in_specs=[
            tok(0), tok(1), tok(5), tok(0), tok(0), layer(ws), layer(bs),
            pl.BlockSpec((tm, 2 * dq), lambda i: (i, 0)),
            pl.BlockSpec((None, dq, d), lambda i: (li, 0, 0)),
            pl.BlockSpec((None, dq, d), lambda i: (li, 1, 0)),
            pl.BlockSpec((None, 2 * dq, d), lambda i: (li, 1, 0)),
            pl.BlockSpec((tm, d), lambda i: (i, 0)),
            pl.BlockSpec((None, None, None, 1, d), lambda i: (li, row, 2, 0, 0)),
        ],
        out_specs=pl.BlockSpec((tm, d), lambda i: (i, 0)),
        scratch_shapes=[pltpu.VMEM((tm, dq), BF16), pltpu.VMEM((tm, dq), BF16)],
        compiler_params=_cparams(("parallel",), [((tm, d), BF16), ((d, d), BF16), ((tm, d), F32), ((tm, d), F32),
                                                 ((tm, dq), F32), ((tm, dq), F32)], [((tm, d), BF16)]),
        name="outproj",
    )(p, p, p, o_f, o_b, ws, bs, yc, w, w, w, x, mods)


def _ffn_kernel(x_ref, sh_ref, sc_ref, gate_ref, g_ref, wg_ref, wu_ref, wd_ref, fg_ref, o_ref, h_ref,
                *, final_norm):
    f = pl.program_id(1)

    @pl.when(f == 0)
    def _():
        h_ref[...] = _modulated_norm(x_ref[...], g_ref[...], sh_ref[...], sc_ref[...]).astype(BF16)
        o_ref[...] = jnp.zeros_like(o_ref)

    h = h_ref[...]
    a = jnp.dot(h, wg_ref[...].astype(BF16), preferred_element_type=F32)
    b = jnp.dot(h, wu_ref[...].astype(BF16), preferred_element_type=F32)
    act = (a * jax.nn.sigmoid(a) * b).astype(BF16)
    o_ref[...] += jnp.dot(act, wd_ref[...].astype(BF16), preferred_element_type=F32)

    @pl.when(f == pl.num_programs(1) - 1)
    def _():
        y = x_ref[...] + gate_ref[...] * o_ref[...]
        if final_norm:
            y = y * lax.rsqrt(jnp.mean(y * y, axis=-1, keepdims=True) + EPS) * fg_ref[...]
        o_ref[...] = y


def _ffn(x, mods, li, row, g, wg, wu, wd, final_g, final_norm):
    m, d = x.shape
    ff = wg.shape[2]
    tm = _tile(m, 1024)
    tf = _tile(ff, 256)
    mod = lambda c: pl.BlockSpec((None, None, None, 1, d), lambda i, f: (li, row, c, 0, 0))
    return pl.pallas_call(
        functools.partial(_ffn_kernel, final_norm=final_norm),
        out_shape=jax.ShapeDtypeStruct((m, d), F32),
        grid=(m // tm, ff // tf),
        in_specs=[
            pl.BlockSpec((tm, d), lambda i, f: (i, 0), pipeline_mode=pl.Buffered(1)),
            mod(3), mod(4), mod(5),
            pl.BlockSpec((None, 1, d), lambda i, f: (li, 0, 0)),
            pl.BlockSpec((None, d, tf), lambda i, f: (li, 0, f)),
            pl.BlockSpec((None, d, tf), lambda i, f: (li, 0, f)),
            pl.BlockSpec((None, tf, d), lambda i, f: (li, f, 0)),
            pl.BlockSpec((1, d), lambda i, f: (0, 0)),
        ],
        out_specs=pl.BlockSpec((tm, d), lambda i, f: (i, 0)),
        scratch_shapes=[pltpu.VMEM((tm, d), BF16)],
        compiler_params=_cparams(("parallel", "arbitrary"),
                                 [((tm, d), F32), ((tm, d), F32)] + [((d, tf), wg.dtype)] * 3, [((tm, d), BF16)]),
        name="ffn",
    )(x, mods, mods, mods, g, wg, wu, wd, final_g)


def _rope_tables(n_rows, identity):
    def one(width):
        half = width // 2
        lane = np.arange(HEAD)
        sub = lane // width
        idx = lane % width
        first = idx < half
        freqs = np.float32(ROPE_BASE) ** (-(idx % half).astype(np.float32) / np.float32(half))
        t = np.arange(n_rows)
        pos = np.where((sub % 2) == 0, (t // GRID_W)[:, None], (t % GRID_W)[:, None]).astype(np.float32)
        ang = pos * freqs[None, :]
        cos, sin = np.cos(ang), np.sin(ang)
        if identity:
            cos, sin = np.ones_like(cos), np.zeros_like(sin)
        return [cos, np.where(first[None, :], -sin, 0.0), np.where(first[None, :], 0.0, sin)]
    return jnp.asarray(np.stack(one(HEAD // 2) + one(HEAD // 4)).astype(np.float32))


def kernel(x, c, ctx, c_ctx, w_ada, b_ada, norm1_g, w_in, sgu_w, sgu_b, ret_decay_fwd, ret_decay_bwd,
           diff_lambda_q1, diff_lambda_k1, diff_lambda_q2, diff_lambda_k2, diff_subln_g, w_out, norm2_g,
           w_gate, w_up, w_down, final_g):
    assert x.shape[0] == 1 and ctx.shape[0] == 1
    depth = w_ada.shape[0]
    xl, xc = x[0], ctx[0]
    n_lat, d = xl.shape
    n_ctx = xc.shape[0]
    total = n_lat + n_ctx
    dq = d // 4

    mods = _ada(jnp.concatenate([c, c_ctx[None, :]], axis=0).T, w_ada, b_ada)
    mods = mods.reshape(depth, 2, N_MOD, 1, d)
    tabs_lat = _rope_tables(n_lat, identity=False)
    tabs_ctx = _rope_tables(n_ctx, identity=True)
    w_out_b = w_out.astype(BF16)
    sgu_w_b = sgu_w.astype(BF16)
    sgu_b_col = sgu_b[..., None]
    g1, g2, fg = norm1_g[:, None, :], norm2_g[:, None, :], final_g[None, :]
    dec = jnp.stack([ret_decay_fwd, ret_decay_bwd]).astype(F32)
    lam_vecs = jnp.stack([diff_lambda_q1, diff_lambda_k1, diff_lambda_q2, diff_lambda_k2], axis=1).astype(F32)
    sub_g = diff_subln_g[:, None, :]

    for li in range(depth):
        need_ctx = li < depth - 1
        last = li == depth - 1
        lam_init = 0.8 - 0.6 * math.exp(-0.3 * li)
        p = _inproj(xl, mods, li, 0, g1, w_in, tabs_lat, total, 0)
        p = _inproj(xc, mods, li, 1, g1, w_in, tabs_ctx, total, n_lat, p_prev=p)
        o_f, o_b = _retention(p, dec, li, n_lat, n_ctx, dq)
        y_c = _attention(p, lam_vecs, sub_g, li, 0, n_lat, 0, total, d, lam_init)
        xl1 = _outproj(p, o_f, o_b, sgu_w_b, sgu_b_col, y_c, 0, w_out_b, xl, mods, li, 0)
        xl_new = _ffn(xl1, mods, li, 0, g2, w_gate, w_up, w_down, fg, last)
        if need_ctx:
            yc_c = _attention(p, lam_vecs, sub_g, li, n_lat, n_ctx, n_lat, n_ctx, d, lam_init)
            xc1 = _outproj(p, o_f, o_b, sgu_w_b, sgu_b_col, yc_c, n_lat, w_out_b, xc, mods, li, 1)
            xc = _ffn(xc1, mods, li, 1, g2, w_gate, w_up, w_down, fg, False)
        xl = xl_new
    return xl[None]
```

```python
import functools
import math

import jax
import jax.numpy as jnp
import numpy as np
from jax import lax
from jax.experimental import pallas as pl
from jax.experimental.pallas import tpu as pltpu

GRID_W = 64
ROPE_BASE = 10000.0
EPS = 1e-6
N_MOD = 6
HEAD = 128
SGU_CHUNK = 128
RET_BLOCK = 256
IN_TILES = 6
ROW_CHUNK = 256
V7X_VMEM_BYTES = 64 * 1024 * 1024
VMEM_RESERVE_BYTES = 2 * 1024 * 1024

F32 = jnp.float32
BF16 = jnp.bfloat16


def _nbytes(shape, dtype):
    return math.prod(shape) * jnp.dtype(dtype).itemsize


def _cparams(sem, windows, scratch=()):
    est = 2 * sum(_nbytes(*w) for w in windows) + sum(_nbytes(*b) for b in scratch)
    limit = min(V7X_VMEM_BYTES - VMEM_RESERVE_BYTES, 2 * est + 16 * 1024 * 1024)
    return pltpu.CompilerParams(dimension_semantics=sem, vmem_limit_bytes=limit)


def _tile(n, want):
    if n <= want:
        return n
    t = want
    while n % t:
        t -= 8
    return t


def _ada_kernel(a_ref, w_ref, b_ref, o_ref, ab_ref, *, rows):
    d, tn = w_ref.shape[1], w_ref.shape[2]

    @pl.when((pl.program_id(0) == 0) & (pl.program_id(1) == 0))
    def _():
        a = a_ref[...]
        a = a * jax.nn.sigmoid(a)
        ab_ref[0] = jnp.broadcast_to(a[:, 0:1], (d, HEAD))
        ab_ref[1] = jnp.broadcast_to(a[:, 1:2], (d, HEAD))

    def body(r, acc):
        r0 = pl.multiple_of(r * rows, rows)
        a0 = ab_ref[0, pl.ds(r0, rows), :]
        a1 = ab_ref[1, pl.ds(r0, rows), :]
        new = []
        for c in range(tn // HEAD):
            w = w_ref[0, pl.ds(r0, rows), c * HEAD:(c + 1) * HEAD]
            new.append(acc[2 * c] + (w * a0).reshape(rows // 8, 8, HEAD).sum(axis=0))
            new.append(acc[2 * c + 1] + (w * a1).reshape(rows // 8, 8, HEAD).sum(axis=0))
        return tuple(new)

    z = jnp.zeros((8, HEAD), F32)
    acc = lax.fori_loop(0, d // rows, body, (z,) * (2 * (tn // HEAD)))
    for c in range(tn // HEAD):
        out = jnp.concatenate([acc[2 * c].sum(axis=0, keepdims=True), acc[2 * c + 1].sum(axis=0, keepdims=True)],
                              axis=0)
        o_ref[0, :, c * HEAD:(c + 1) * HEAD] = out + b_ref[0, :, c * HEAD:(c + 1) * HEAD]


def _ada(c2t, w_ada, b_ada):
    depth, d, n = w_ada.shape
    tn = _tile(n, 1024)
    return pl.pallas_call(
        functools.partial(_ada_kernel, rows=128),
        out_shape=jax.ShapeDtypeStruct((depth, 2, n), F32),
        grid=(depth, n // tn),
        in_specs=[
            pl.BlockSpec((d, 2), lambda l, j: (0, 0)),
            pl.BlockSpec((1, d, tn), lambda l, j: (l, 0, j)),
            pl.BlockSpec((1, 1, tn), lambda l, j: (l, 0, j)),
        ],
        out_specs=pl.BlockSpec((1, 2, tn), lambda l, j: (l, 0, j)),
        scratch_shapes=[pltpu.VMEM((2, d, HEAD), F32)],
        compiler_params=_cparams(("arbitrary", "arbitrary"), [((d, tn), F32), ((d, HEAD), F32)],
                                 [((2, d, HEAD), F32)]),
        name="ada",
    )(c2t, w_ada, b_ada.reshape(depth, 1, n))


def _rope(x, c, s1, s2, shift):
    return x * c + pltpu.roll(x, HEAD - shift, 1) * s1 + pltpu.roll(x, shift, 1) * s2


def _gelu_tanh(x):
    c = 2.0 * math.sqrt(2.0 / math.pi)
    return x * jax.nn.sigmoid(x * (c + (c * 0.044715) * (x * x)))


def _group_rms(x):
    return x * lax.rsqrt(jnp.mean(x * x, axis=-1, keepdims=True) + EPS)


def _modulated_norm(x, g, sh, sc):
    y = x * lax.rsqrt(jnp.mean(x * x, axis=-1, keepdims=True) + EPS) * g
    return y * (1.0 + sc) + sh


def _inproj_kernel(x_ref, sh_ref, sc_ref, g_ref, w_ref, tab_ref, *rest, aliased):
    o_ref, h_ref = rest[1:] if aliased else rest
    j = pl.program_id(1)
    tn = w_ref.shape[1]

    @pl.when(j == 0)
    def _():
        h_ref[...] = _modulated_norm(x_ref[...], g_ref[...], sh_ref[...], sc_ref[...]).astype(BF16)

    def project(*fns):
        rc = min(h_ref.shape[0], ROW_CHUNK)
        part = tn // len(fns)
        slab = min(2 * HEAD, part)
        for c0 in range(0, tn, slab):
            fn = fns[c0 // part]
            w = w_ref[:, c0:c0 + slab].astype(BF16)
            for r in range(h_ref.shape[0] // rc):
                rs = slice(r * rc, (r + 1) * rc)
                acc = jnp.dot(h_ref[rs, :], w, preferred_element_type=F32)
                for hh in range(slab // HEAD):
                    o_ref[rs, c0 + hh * HEAD:c0 + (hh + 1) * HEAD] = fn(
                        acc[:, hh * HEAD:(hh + 1) * HEAD], rs).astype(BF16)

    def rope_ret(scale):
        return lambda a, rs: _rope(a, tab_ref[0, rs, :], tab_ref[1, rs, :], tab_ref[2, rs, :], HEAD // 4) * scale

    def rope_diff(scale):
        return lambda a, rs: _rope(a, tab_ref[3, rs, :], tab_ref[4, rs, :], tab_ref[5, rs, :], HEAD // 8) * scale

    @pl.when(j == 0)
    def _():
        project(lambda a, rs: _gelu_tanh(a), lambda a, rs: _group_rms(_gelu_tanh(a)))

    @pl.when(j == 1)
    def _():
        project(rope_ret(1.0), rope_ret(HEAD ** -0.5))

    @pl.when(j == 2)
    def _():
        project(lambda a, rs: a, lambda a, rs: a * jax.nn.sigmoid(a))

    @pl.when(j == 3)
    def _():
        project(rope_diff((HEAD // 2) ** -0.5 * math.log2(math.e)))

    @pl.when(j == 4)
    def _():
        project(rope_diff(1.0))

    @pl.when(j == 5)
    def _():
        project(lambda a, rs: a)


def _inproj(x, mods, li, row, g, w, tabs, total_rows, row_off, p_prev=None):
    m, d = x.shape
    n = w.shape[2]
    tn = n // IN_TILES
    tm = _tile(m, 1024)
    assert row_off % tm == 0
    ob = row_off // tm
    aliased = p_prev is not None
    in_specs = [
        pl.BlockSpec((tm, d), lambda i, j: (i, 0)),
        pl.BlockSpec((None, None, None, 1, d), lambda i, j: (li, row, 0, 0, 0)),
        pl.BlockSpec((None, None, None, 1, d), lambda i, j: (li, row, 1, 0, 0)),
        pl.BlockSpec((None, 1, d), lambda i, j: (li, 0, 0)),
        pl.BlockSpec((None, d, tn), lambda i, j: (li, 0, j)),
        pl.BlockSpec((6, tm, HEAD), lambda i, j: (0, i, 0)),
    ]
    args = [x, mods, mods, g, w, tabs]
    if aliased:
        in_specs.append(pl.BlockSpec(memory_space=pl.ANY))
        args.append(p_prev)
    return pl.pallas_call(
        functools.partial(_inproj_kernel, aliased=aliased),
        out_shape=jax.ShapeDtypeStruct((total_rows, n), BF16),
        grid=(m // tm, IN_TILES),
        in_specs=in_specs,
        out_specs=pl.BlockSpec((tm, tn), lambda i, j: (ob + i, j)),
        scratch_shapes=[pltpu.VMEM((tm, d), BF16)],
        input_output_aliases={6: 0} if aliased else {},
        compiler_params=_cparams(("parallel", "arbitrary"),
                                 [((tm, d), x.dtype), ((d, tn), w.dtype), ((6, tm, HEAD), F32), ((tm, tn), BF16)],
                                 [((tm, d), BF16)]),
        name="inproj",
    )(*args)


def _log_sigmoid(z):
    return jnp.minimum(z, 0.0) - jnp.log1p(jnp.exp(-jnp.abs(z)))


def _ret_kernel(dec_ref, qf_ref, kf_ref, vf_ref, qb_ref, kb_ref, vb_ref, of_ref, ob_ref,
                state_ref, mask_ref, qd_ref, kd_ref, cd_ref, *, li):
    s = pl.program_id(0)
    rc = qf_ref.shape[0]
    nh = qf_ref.shape[1] // HEAD

    @pl.when(s == 0)
    def _():
        state_ref[...] = jnp.zeros_like(state_ref)
        ii = lax.broadcasted_iota(jnp.int32, (rc, rc), 0).astype(F32)
        jj = lax.broadcasted_iota(jnp.int32, (rc, rc), 1).astype(F32)
        col = lax.broadcasted_iota(jnp.int32, (rc, HEAD), 0).astype(F32)
        for dr in range(2):
            rel = (ii - jj) if dr == 0 else (jj - ii)
            for h in range(nh):
                logit = dec_ref[dr, li, h]
                lg = _log_sigmoid(jnp.full((rc, rc), logit, F32))
                mask_ref[dr, h] = jnp.where(rel >= 0, jnp.exp(jnp.maximum(rel, 0.0) * lg), 0.0)
                lgc = _log_sigmoid(jnp.full((rc, HEAD), logit, F32))
                q_pow = (col + 1.0) if dr == 0 else (rc - col)
                k_pow = (rc - 1.0 - col) if dr == 0 else col
                qd_ref[dr, h] = jnp.exp(q_pow * lgc)
                kd_ref[dr, h] = jnp.exp(k_pow * lgc)
                cd_ref[dr, h] = jnp.exp(rc * _log_sigmoid(jnp.full((HEAD, HEAD), logit, F32)))

    for dr, (q_ref, k_ref, v_ref, o_ref) in enumerate(
            ((qf_ref, kf_ref, vf_ref, of_ref), (qb_ref, kb_ref, vb_ref, ob_ref))):
        for h in range(nh):
            sl = slice(h * HEAD, (h + 1) * HEAD)
            q = q_ref[:, sl]
            k = k_ref[:, sl]
            v = v_ref[:, sl]
            state = state_ref[dr, h]
            scores = lax.dot_general(q, k, (((1,), (1,)), ((), ())), preferred_element_type=F32)
            scores = (scores * mask_ref[dr, h]).astype(BF16)
            qs = (q.astype(F32) * qd_ref[dr, h]).astype(BF16)
            out = jnp.dot(scores, v, preferred_element_type=F32)
            out = out + jnp.dot(qs, state.astype(BF16), preferred_element_type=F32)
            o_ref[:, sl] = out
            ks = (k.astype(F32) * kd_ref[dr, h]).astype(BF16)
            upd = lax.dot_general(ks, v, (((0,), (0,)), ((), ())), preferred_element_type=F32)
            state_ref[dr, h] = state * cd_ref[dr, h] + upd


def _retention(p, dec, li, n_lat, n_ctx, d_ret):
    t = p.shape[0]
    rc = RET_BLOCK
    assert n_lat % rc == 0 and n_ctx % rc == 0
    nb, nlb, ncb = t // rc, n_lat // rc, n_ctx // rc
    nh = d_ret // HEAD
    cq, ck, cv = 2, 3, 4

    def fmap(col):
        return lambda s: (jnp.where(s < ncb, nlb + s, s - ncb), col)

    def bmap(col):
        return lambda s: (nb - 1 - s, col)

    blk = lambda im: pl.BlockSpec((rc, d_ret), im)
    return pl.pallas_call(
        functools.partial(_ret_kernel, li=li),
        out_shape=[jax.ShapeDtypeStruct((t, d_ret), F32)] * 2,
        grid=(nb,),
        in_specs=[pl.BlockSpec(memory_space=pltpu.SMEM),
                  blk(fmap(cq)), blk(fmap(ck)), blk(fmap(cv)),
                  blk(bmap(cq)), blk(bmap(ck)), blk(bmap(cv))],
        out_specs=[blk(fmap(0)), blk(bmap(0))],
        scratch_shapes=[
            pltpu.VMEM((2, nh, HEAD, HEAD), F32),
            pltpu.VMEM((2, nh, rc, rc), F32),
            pltpu.VMEM((2, nh, rc, HEAD), F32),
            pltpu.VMEM((2, nh, rc, HEAD), F32),
            pltpu.VMEM((2, nh, HEAD, HEAD), F32),
        ],
        compiler_params=_cparams(("arbitrary",), [((rc, d_ret), BF16)] * 6 + [((rc, d_ret), F32)] * 2,
                                 [((2, nh, rc, rc), F32), ((2, nh, rc, HEAD), F32), ((2, nh, rc, HEAD), F32)]),
        name="retention",
    )(dec, p, p, p, p, p, p)


N_BUF = 4
SM_LAG = 1
PV_LAG = 3
KEY_CHUNK = 1024
FIRST_KEY_CHUNK = 256


def _key_chunks(n_k):
    if n_k <= KEY_CHUNK:
        return [(0, n_k)]
    chunks, r0 = [(0, FIRST_KEY_CHUNK)], FIRST_KEY_CHUNK
    while r0 < n_k:
        size = min(KEY_CHUNK, n_k - r0)
        chunks.append((r0, size))
        r0 += size
    return chunks


def _attn_kernel(q_ref, k_ref, v_ref, lam_ref, g_ref, o_ref, *scratch, lam_init):
    tq = q_ref.shape[0]
    chunks = _key_chunks(k_ref.shape[0])
    nkv = len(chunks)
    s_bufs, p_bufs, a_bufs, pm_bufs = (scratch[i * N_BUF:(i + 1) * N_BUF] for i in range(4))
    acc_ref, m_ref, qs_ref = scratch[4 * N_BUF:]
    q = q_ref[...]
    lane = lax.broadcasted_iota(jnp.int32, q.shape, 1)
    zero = jnp.zeros_like(q)
    qs_ref[:tq] = jnp.where(lane < HEAD // 2, q, zero)
    qs_ref[tq:] = jnp.where(lane >= HEAD // 2, q, zero)
    acc_ref[...] = jnp.zeros_like(acc_ref)
    m_ref[...] = jnp.full(m_ref.shape, -jnp.inf, F32)

    def scores(c):
        r0, size = chunks[c]
        s = lax.dot_general(qs_ref[...], k_ref[r0:r0 + size, :], (((1,), (1,)), ((), ())),
                            preferred_element_type=F32)
        s_bufs[c % N_BUF][:, :size] = s
        pm = s[:, :HEAD]
        for j in range(1, size // HEAD):
            pm = jnp.maximum(pm, s[:, j * HEAD:(j + 1) * HEAD])
        pm_bufs[c % N_BUF][...] = pm

    def softmax(c):
        size = chunks[c][1]
        s_ref, p_ref = s_bufs[c % N_BUF], p_bufs[c % N_BUF]
        m_old = m_ref[...]
        m_new = jnp.maximum(m_old, jnp.broadcast_to(pm_bufs[c % N_BUF][...].max(axis=1, keepdims=True),
                                                    m_old.shape))
        a_bufs[c % N_BUF][...] = jnp.exp2(m_old - m_new)
        m_ref[...] = m_new
        for j in range(size // HEAD):
            sl = slice(j * HEAD, (j + 1) * HEAD)
            p_ref[:, sl] = jnp.exp2(s_ref[:, sl] - m_new).astype(BF16)

    def weighted_values(c):
        r0, size = chunks[c]
        v_ext = jnp.concatenate([v_ref[r0:r0 + size, :], jnp.ones((size, HEAD), BF16)], axis=1)
        pv = jnp.dot(p_bufs[c % N_BUF][:, :size], v_ext, preferred_element_type=F32)
        alpha = a_bufs[c % N_BUF][...]
        acc_ref[:, :HEAD] = alpha * acc_ref[:, :HEAD] + pv[:, :HEAD]
        acc_ref[:, HEAD:] = alpha * acc_ref[:, HEAD:] + pv[:, HEAD:]

    for t in range(nkv + PV_LAG):
        if t < nkv:
            scores(t)
        if 0 <= t - SM_LAG < nkv:
            softmax(t - SM_LAG)
        if 0 <= t - PV_LAG < nkv:
            weighted_values(t - PV_LAG)

    lv = lam_ref[...]
    lam = (jnp.exp(jnp.sum(lv[0:1] * lv[1:2], axis=1, keepdims=True))
           - jnp.exp(jnp.sum(lv[2:3] * lv[3:4], axis=1, keepdims=True)) + lam_init)
    on = acc_ref[:, :HEAD] / acc_ref[:, HEAD:]
    o = on[:tq] - lam * on[tq:]
    y = o * lax.rsqrt(jnp.mean(o * o, axis=-1, keepdims=True) + EPS) * g_ref[...]
    o_ref[...] = (y * (1.0 - lam_init)).astype(BF16)


def _attention(p, lam_vecs, subln_g, li, q_row0, n_q, k_row0, n_k, d_model, lam_init):
    d_diff = d_model // 2
    nh = d_diff // HEAD
    quarter = (d_model // 4) // HEAD
    cq, ck, cv = 6 * quarter, 8 * quarter, 10 * quarter
    tq = _tile(n_q, 512)
    tk = max(size for _, size in _key_chunks(n_k))
    assert n_k % HEAD == 0
    assert q_row0 % tq == 0 and k_row0 % n_k == 0
    qb, kb = q_row0 // tq, k_row0 // n_k
    return pl.pallas_call(
        functools.partial(_attn_kernel, lam_init=lam_init),
        out_shape=jax.ShapeDtypeStruct((n_q, d_diff), BF16),
        grid=(nh, n_q // tq),
        in_specs=[
            pl.BlockSpec((tq, HEAD), lambda h, i: (qb + i, cq + h)),
            pl.BlockSpec((n_k, HEAD), lambda h, i: (kb, ck + h)),
            pl.BlockSpec((n_k, HEAD), lambda h, i: (kb, cv + h)),
            pl.BlockSpec((None,) + lam_vecs.shape[1:], lambda h, i: (li, 0, 0)),
            pl.BlockSpec((None, 1, HEAD), lambda h, i: (li, 0, 0)),
        ],
        out_specs=pl.BlockSpec((tq, HEAD), lambda h, i: (i, h)),
        scratch_shapes=([pltpu.VMEM((2 * tq, tk), F32)] * N_BUF + [pltpu.VMEM((2 * tq, tk), BF16)] * N_BUF
                        + [pltpu.VMEM((2 * tq, HEAD), F32)] * (2 * N_BUF)
                        + [pltpu.VMEM((2 * tq, 2 * HEAD), F32), pltpu.VMEM((2 * tq, HEAD), F32),
                           pltpu.VMEM((2 * tq, HEAD), BF16)]),
        compiler_params=_cparams(("parallel", "parallel"), [((n_k, HEAD), BF16)] * 2 + [((tq, HEAD), BF16)] * 2,
                                 [((2 * tq, tk), F32)] * N_BUF + [((2 * tq, tk), BF16)] * N_BUF
                                 + [((2 * tq, HEAD), F32)] * (2 * N_BUF + 4)),
        name="diffattn",
    )(p, p, p, lam_vecs, subln_g)


def _outproj_kernel(u_ref, v_ref, gate_ref, of_ref, ob_ref, ws_ref, bs_ref, yc_ref, wa_ref, wb_ref, wc_ref,
                    x_ref, g1_ref, o_ref, ya_ref, yb_ref):
    tm = x_ref.shape[0]
    for n in range(tm // SGU_CHUNK):
        rs = slice(n * SGU_CHUNK, (n + 1) * SGU_CHUNK)
        for g in range(u_ref.shape[1] // HEAD):
            cs = slice(g * HEAD, (g + 1) * HEAD)
            mixed = jnp.dot(ws_ref[g], v_ref[rs, cs], preferred_element_type=F32) + bs_ref[g]
            ya_ref[rs, cs] = (u_ref[rs, cs].astype(F32) * mixed).astype(BF16)
    for h in range(of_ref.shape[1] // HEAD):
        cs = slice(h * HEAD, (h + 1) * HEAD)
        o = of_ref[:, cs] + ob_ref[:, cs]
        yb_ref[:, cs] = (_group_rms(o) * gate_ref[:, cs].astype(F32)).astype(BF16)
    acc = jnp.dot(ya_ref[...], wa_ref[...], preferred_element_type=F32)
    acc = acc + jnp.dot(yb_ref[...], wb_ref[...], preferred_element_type=F32)
    acc = acc + jnp.dot(yc_ref[...], wc_ref[...], preferred_element_type=F32)
    o_ref[...] = x_ref[...] + g1_ref[...] * acc


def _outproj(p, o_f, o_b, ws, bs, yc, y_row0, w, x, mods, li, row):
    m, d = x.shape
    dq = d // 4
    tm = _tile(m, 512)
    assert y_row0 % tm == 0 and tm % SGU_CHUNK == 0
    yb0 = y_row0 // tm
    tok = lambda col: pl.BlockSpec((tm, dq), lambda i: (yb0 + i, col))
    layer = lambda a: pl.BlockSpec((None,) + a.shape[1:], lambda i: (li,) + (0,) * (a.ndim - 1))
    return pl.pallas_call(
        _outproj_kernel,
        out_shape=jax.ShapeDtypeStruct((m, d), F32),
        grid=(m // tm,),
        in_specs=[
            tok(0), tok(1), tok(5), tok(0), tok(0), layer(ws), layer(bs),
            pl.BlockSpec((tm, 2 * dq), lambda i: (i, 0)),
            pl.BlockSpec((None, dq, d), lambda i: (li, 0, 0)),
            pl.BlockSpec((None, dq, d), lambda i: (li, 1, 0)),
            pl.BlockSpec((None, 2 * dq, d), lambda i: (li, 1, 0)),
            pl.BlockSpec((tm, d), lambda i: (i, 0)),
            pl.BlockSpec((None, None, None, 1, d), lambda i: (li, row, 2, 0, 0)),
        ],
        out_specs=pl.BlockSpec((tm, d), lambda i: (i, 0)),
        scratch_shapes=[pltpu.VMEM((tm, dq), BF16), pltpu.VMEM((tm, dq), BF16)],
        compiler_params=_cparams(("parallel",), [((tm, d), BF16), ((d, d), BF16), ((tm, d), F32), ((tm, d), F32),
                                                 ((tm, dq), F32), ((tm, dq), F32)], [((tm, d), BF16)]),
        name="outproj",
    )(p, p, p, o_f, o_b, ws, bs, yc, w, w, w, x, mods)


def _ffn_kernel(x_ref, sh_ref, sc_ref, gate_ref, g_ref, wg_ref, wu_ref, wd_ref, fg_ref, o_ref, h_ref,
                *, final_norm):
    f = pl.program_id(1)

    def swiglu_slice(first):
        h = h_ref[...]
        a = jnp.dot(h, wg_ref[...].astype(BF16), preferred_element_type=F32)
        b = jnp.dot(h, wu_ref[...].astype(BF16), preferred_element_type=F32)
        act = (a * jax.nn.sigmoid(a) * b).astype(BF16)
        upd = jnp.dot(act, wd_ref[...].astype(BF16), preferred_element_type=F32)
        if first:
            o_ref[...] = upd
        else:
            o_ref[...] += upd

    @pl.when(f == 0)
    def _():
        h_ref[...] = _modulated_norm(x_ref[...], g_ref[...], sh_ref[...], sc_ref[...]).astype(BF16)
        swiglu_slice(True)

    @pl.when(f != 0)
    def _():
        swiglu_slice(False)

    @pl.when(f == pl.num_programs(1) - 1)
    def _():
        y = x_ref[...] + gate_ref[...] * o_ref[...]
        if final_norm:
            y = y * lax.rsqrt(jnp.mean(y * y, axis=-1, keepdims=True) + EPS) * fg_ref[...]
        o_ref[...] = y


def _ffn(x, mods, li, row, g, wg, wu, wd, final_g, final_norm):
    m, d = x.shape
    ff = wg.shape[2]
    tm = _tile(m, 1024)
    tf = _tile(ff, 256)
    mod = lambda c: pl.BlockSpec((None, None, None, 1, d), lambda i, f: (li, row, c, 0, 0))
    return pl.pallas_call(
        functools.partial(_ffn_kernel, final_norm=final_norm),
        out_shape=jax.ShapeDtypeStruct((m, d), F32),
        grid=(m // tm, ff // tf),
        in_specs=[
            pl.BlockSpec((tm, d), lambda i, f: (i, 0), pipeline_mode=pl.Buffered(1)),
            mod(3), mod(4), mod(5),
            pl.BlockSpec((None, 1, d), lambda i, f: (li, 0, 0)),
            pl.BlockSpec((None, d, tf), lambda i, f: (li, 0, f)),
            pl.BlockSpec((None, d, tf), lambda i, f: (li, 0, f)),
            pl.BlockSpec((None, tf, d), lambda i, f: (li, f, 0)),
            pl.BlockSpec((1, d), lambda i, f: (0, 0)),
        ],
        out_specs=pl.BlockSpec((tm, d), lambda i, f: (i, 0)),
        scratch_shapes=[pltpu.VMEM((tm, d), BF16)],
        compiler_params=_cparams(("parallel", "arbitrary"),
                                 [((tm, d), F32), ((tm, d), F32)] + [((d, tf), wg.dtype)] * 3, [((tm, d), BF16)]),
        name="ffn",
    )(x, mods, mods, mods, g, wg, wu, wd, final_g)


def _rope_tables(n_rows, identity):
    def one(width):
        half = width // 2
        lane = np.arange(HEAD)
        sub = lane // width
        idx = lane % width
        first = idx < half
        freqs = np.float32(ROPE_BASE) ** (-(idx % half).astype(np.float32) / np.float32(half))
        t = np.arange(n_rows)
        pos = np.where((sub % 2) == 0, (t // GRID_W)[:, None], (t % GRID_W)[:, None]).astype(np.float32)
        ang = pos * freqs[None, :]
        cos, sin = np.cos(ang), np.sin(ang)
        if identity:
            cos, sin = np.ones_like(cos), np.zeros_like(sin)
        return [cos, np.where(first[None, :], -sin, 0.0), np.where(first[None, :], 0.0, sin)]
    return jnp.asarray(np.stack(one(HEAD // 2) + one(HEAD // 4)).astype(np.float32))


def kernel(x, c, ctx, c_ctx, w_ada, b_ada, norm1_g, w_in, sgu_w, sgu_b, ret_decay_fwd, ret_decay_bwd,
           diff_lambda_q1, diff_lambda_k1, diff_lambda_q2, diff_lambda_k2, diff_subln_g, w_out, norm2_g,
           w_gate, w_up, w_down, final_g):
    assert x.shape[0] == 1 and ctx.shape[0] == 1
    depth = w_ada.shape[0]
    xl, xc = x[0], ctx[0]
    n_lat, d = xl.shape
    n_ctx = xc.shape[0]
    total = n_lat + n_ctx
    dq = d // 4

    mods = _ada(jnp.concatenate([c, c_ctx[None, :]], axis=0).T, w_ada, b_ada)
    mods = mods.reshape(depth, 2, N_MOD, 1, d)
    tabs_lat = _rope_tables(n_lat, identity=False)
    tabs_ctx = _rope_tables(n_ctx, identity=True)
    w_out_b = w_out.astype(BF16)
    sgu_w_b = sgu_w.astype(BF16)
    sgu_b_col = sgu_b[..., None]
    g1, g2, fg = norm1_g[:, None, :], norm2_g[:, None, :], final_g[None, :]
    dec = jnp.stack([ret_decay_fwd, ret_decay_bwd]).astype(F32)
    lam_vecs = jnp.stack([diff_lambda_q1, diff_lambda_k1, diff_lambda_q2, diff_lambda_k2], axis=1).astype(F32)
    sub_g = diff_subln_g[:, None, :]

    for li in range(depth):
        need_ctx = li < depth - 1
        last = li == depth - 1
        lam_init = 0.8 - 0.6 * math.exp(-0.3 * li)
        p = _inproj(xl, mods, li, 0, g1, w_in, tabs_lat, total, 0)
        p = _inproj(xc, mods, li, 1, g1, w_in, tabs_ctx, total, n_lat, p_prev=p)
        o_f, o_b = _retention(p, dec, li, n_lat, n_ctx, dq)
        y_c = _attention(p, lam_vecs, sub_g, li, 0, n_lat, 0, total, d, lam_init)
        xl1 = _outproj(p, o_f, o_b, sgu_w_b, sgu_b_col, y_c, 0, w_out_b, xl, mods, li, 0)
        xl_new = _ffn(xl1, mods, li, 0, g2, w_gate, w_up, w_down, fg, last)
        if need_ctx:
            yc_c = _attention(p, lam_vecs, sub_g, li, n_lat, n_ctx, n_lat, n_ctx, d, lam_init)
            xc1 = _outproj(p, o_f, o_b, sgu_w_b, sgu_b_col, yc_c, n_lat, w_out_b, xc, mods, li, 1)
            xc = _ffn(xc1, mods, li, 1, g2, w_gate, w_up, w_down, fg, False)
        xl = xl_new
    return xl[None]
```

```python
import functools
import math

import jax
import jax.numpy as jnp
import numpy as np
from jax import lax
from jax.experimental import pallas as pl
from jax.experimental.pallas import tpu as pltpu

GRID_W = 64
ROPE_BASE = 10000.0
EPS = 1e-6
N_MOD = 6
HEAD = 128
SGU_CHUNK = 128
RET_BLOCK = 256
IN_TILES = 6
ROW_CHUNK = 256
V7X_VMEM_BYTES = 64 * 1024 * 1024
VMEM_RESERVE_BYTES = 2 * 1024 * 1024

F32 = jnp.float32
BF16 = jnp.bfloat16


def _nbytes(shape, dtype):
    return math.prod(shape) * jnp.dtype(dtype).itemsize


def _cparams(sem, windows, scratch=()):
    est = 2 * sum(_nbytes(*w) for w in windows) + sum(_nbytes(*b) for b in scratch)
    limit = min(V7X_VMEM_BYTES - VMEM_RESERVE_BYTES, 2 * est + 16 * 1024 * 1024)
    return pltpu.CompilerParams(dimension_semantics=sem, vmem_limit_bytes=limit)


def _tile(n, want):
    if n <= want:
        return n
    t = want
    while n % t:
        t -= 8
    return t


def _ada_kernel(a_ref, w_ref, b_ref, o_ref, ab_ref, *, rows):
    d, tn = w_ref.shape[1], w_ref.shape[2]

    @pl.when((pl.program_id(0) == 0) & (pl.program_id(1) == 0))
    def _():
        a = a_ref[...]
        a = a * jax.nn.sigmoid(a)
        ab_ref[0] = jnp.broadcast_to(a[:, 0:1], (d, HEAD))
        ab_ref[1] = jnp.broadcast_to(a[:, 1:2], (d, HEAD))

    def body(r, acc):
        r0 = pl.multiple_of(r * rows, rows)
        a0 = ab_ref[0, pl.ds(r0, rows), :]
        a1 = ab_ref[1, pl.ds(r0, rows), :]
        new = []
        for c in range(tn // HEAD):
            w = w_ref[0, pl.ds(r0, rows), c * HEAD:(c + 1) * HEAD]
            new.append(acc[2 * c] + (w * a0).reshape(rows // 8, 8, HEAD).sum(axis=0))
            new.append(acc[2 * c + 1] + (w * a1).reshape(rows // 8, 8, HEAD).sum(axis=0))
        return tuple(new)

    z = jnp.zeros((8, HEAD), F32)
    acc = lax.fori_loop(0, d // rows, body, (z,) * (2 * (tn // HEAD)))
    for c in range(tn // HEAD):
        out = jnp.concatenate([acc[2 * c].sum(axis=0, keepdims=True), acc[2 * c + 1].sum(axis=0, keepdims=True)],
                              axis=0)
        o_ref[0, :, c * HEAD:(c + 1) * HEAD] = out + b_ref[0, :, c * HEAD:(c + 1) * HEAD]


def _ada(c2t, w_ada, b_ada):
    depth, d, n = w_ada.shape
    tn = _tile(n, 1024)
    return pl.pallas_call(
        functools.partial(_ada_kernel, rows=128),
        out_shape=jax.ShapeDtypeStruct((depth, 2, n), F32),
        grid=(depth, n // tn),
        in_specs=[
            pl.BlockSpec((d, 2), lambda l, j: (0, 0)),
            pl.BlockSpec((1, d, tn), lambda l, j: (l, 0, j)),
            pl.BlockSpec((1, 1, tn), lambda l, j: (l, 0, j)),
        ],
        out_specs=pl.BlockSpec((1, 2, tn), lambda l, j: (l, 0, j)),
        scratch_shapes=[pltpu.VMEM((2, d, HEAD), F32)],
        compiler_params=_cparams(("arbitrary", "arbitrary"), [((d, tn), F32), ((d, HEAD), F32)],
                                 [((2, d, HEAD), F32)]),
        name="ada",
    )(c2t, w_ada, b_ada.reshape(depth, 1, n))


def _rope(x, c, s1, s2, shift):
    return x * c + pltpu.roll(x, HEAD - shift, 1) * s1 + pltpu.roll(x, shift, 1) * s2


def _gelu_tanh(x):
    c = 2.0 * math.sqrt(2.0 / math.pi)
    return x * jax.nn.sigmoid(x * (c + (c * 0.044715) * (x * x)))


def _group_rms(x):
    return x * lax.rsqrt(jnp.mean(x * x, axis=-1, keepdims=True) + EPS)


def _modulated_norm(x, g, sh, sc):
    y = x * lax.rsqrt(jnp.mean(x * x, axis=-1, keepdims=True) + EPS) * g
    return y * (1.0 + sc) + sh


def _inproj_kernel(x_ref, sh_ref, sc_ref, g_ref, w_ref, tab_ref, *rest, aliased):
    o_ref, h_ref = rest[1:] if aliased else rest
    j = pl.program_id(1)
    tn = w_ref.shape[1]

    @pl.when(j == 0)
    def _():
        h_ref[...] = _modulated_norm(x_ref[...], g_ref[...], sh_ref[...], sc_ref[...]).astype(BF16)

    def project(*fns):
        rc = min(h_ref.shape[0], ROW_CHUNK)
        part = tn // len(fns)
        slab = min(2 * HEAD, part)
        for c0 in range(0, tn, slab):
            fn = fns[c0 // part]
            w = w_ref[:, c0:c0 + slab].astype(BF16)
            for r in range(h_ref.shape[0] // rc):
                rs = slice(r * rc, (r + 1) * rc)
                acc = jnp.dot(h_ref[rs, :], w, preferred_element_type=F32)
                for hh in range(slab // HEAD):
                    o_ref[rs, c0 + hh * HEAD:c0 + (hh + 1) * HEAD] = fn(
                        acc[:, hh * HEAD:(hh + 1) * HEAD], rs).astype(BF16)

    def rope_ret(scale):
        return lambda a, rs: _rope(a, tab_ref[0, rs, :], tab_ref[1, rs, :], tab_ref[2, rs, :], HEAD // 4) * scale

    def rope_diff(scale):
        return lambda a, rs: _rope(a, tab_ref[3, rs, :], tab_ref[4, rs, :], tab_ref[5, rs, :], HEAD // 8) * scale

    @pl.when(j == 0)
    def _():
        project(lambda a, rs: _gelu_tanh(a), lambda a, rs: _group_rms(_gelu_tanh(a)))

    @pl.when(j == 1)
    def _():
        project(rope_ret(1.0), rope_ret(HEAD ** -0.5))

    @pl.when(j == 2)
    def _():
        project(lambda a, rs: a, lambda a, rs: a * jax.nn.sigmoid(a))

    @pl.when(j == 3)
    def _():
        project(rope_diff((HEAD // 2) ** -0.5 * math.log2(math.e)))

    @pl.when(j == 4)
    def _():
        project(rope_diff(1.0))

    @pl.when(j == 5)
    def _():
        project(lambda a, rs: a)


def _inproj(x, mods, li, row, g, w, tabs, total_rows, row_off, p_prev=None):
    m, d = x.shape
    n = w.shape[2]
    tn = n // IN_TILES
    tm = _tile(m, 1024)
    assert row_off % tm == 0
    ob = row_off // tm
    aliased = p_prev is not None
    in_specs = [
        pl.BlockSpec((tm, d), lambda i, j: (i, 0)),
        pl.BlockSpec((None, None, None, 1, d), lambda i, j: (li, row, 0, 0, 0)),
        pl.BlockSpec((None, None, None, 1, d), lambda i, j: (li, row, 1, 0, 0)),
        pl.BlockSpec((None, 1, d), lambda i, j: (li, 0, 0)),
        pl.BlockSpec((None, d, tn), lambda i, j: (li, 0, j)),
        pl.BlockSpec((6, tm, HEAD), lambda i, j: (0, i, 0)),
    ]
    args = [x, mods, mods, g, w, tabs]
    if aliased:
        in_specs.append(pl.BlockSpec(memory_space=pl.ANY))
        args.append(p_prev)
    return pl.pallas_call(
        functools.partial(_inproj_kernel, aliased=aliased),
        out_shape=jax.ShapeDtypeStruct((total_rows, n), BF16),
        grid=(m // tm, IN_TILES),
        in_specs=in_specs,
        out_specs=pl.BlockSpec((tm, tn), lambda i, j: (ob + i, j)),
        scratch_shapes=[pltpu.VMEM((tm, d), BF16)],
        input_output_aliases={6: 0} if aliased else {},
        compiler_params=_cparams(("parallel", "arbitrary"),
                                 [((tm, d), x.dtype), ((d, tn), w.dtype), ((6, tm, HEAD), F32), ((tm, tn), BF16)],
                                 [((tm, d), BF16)]),
        name="inproj",
    )(*args)


def _log_sigmoid(z):
    return jnp.minimum(z, 0.0) - jnp.log1p(jnp.exp(-jnp.abs(z)))


def _ret_kernel(dec_ref, qf_ref, kf_ref, vf_ref, qb_ref, kb_ref, vb_ref, of_ref, ob_ref,
                state_ref, mask_ref, qd_ref, kd_ref, cd_ref, *, li):
    s = pl.program_id(0)
    rc = qf_ref.shape[0]
    nh = qf_ref.shape[1] // HEAD

    @pl.when(s == 0)
    def _():
        state_ref[...] = jnp.zeros_like(state_ref)
        ii = lax.broadcasted_iota(jnp.int32, (rc, rc), 0).astype(F32)
        jj = lax.broadcasted_iota(jnp.int32, (rc, rc), 1).astype(F32)
        col = lax.broadcasted_iota(jnp.int32, (rc, HEAD), 0).astype(F32)
        for dr in range(2):
            rel = (ii - jj) if dr == 0 else (jj - ii)
            for h in range(nh):
                logit = dec_ref[dr, li, h]
                lg = _log_sigmoid(jnp.full((rc, rc), logit, F32))
                mask_ref[dr, h] = jnp.where(rel >= 0, jnp.exp(jnp.maximum(rel, 0.0) * lg), 0.0)
                lgc = _log_sigmoid(jnp.full((rc, HEAD), logit, F32))
                q_pow = (col + 1.0) if dr == 0 else (rc - col)
                k_pow = (rc - 1.0 - col) if dr == 0 else col
                qd_ref[dr, h] = jnp.exp(q_pow * lgc)
                kd_ref[dr, h] = jnp.exp(k_pow * lgc)
                cd_ref[dr, h] = jnp.exp(rc * _log_sigmoid(jnp.full((HEAD, HEAD), logit, F32)))

    for dr, (q_ref, k_ref, v_ref, o_ref) in enumerate(
            ((qf_ref, kf_ref, vf_ref, of_ref), (qb_ref, kb_ref, vb_ref, ob_ref))):
        for h in range(nh):
            sl = slice(h * HEAD, (h + 1) * HEAD)
            q = q_ref[:, sl]
            k = k_ref[:, sl]
            v = v_ref[:, sl]
            state = state_ref[dr, h]
            scores = lax.dot_general(q, k, (((1,), (1,)), ((), ())), preferred_element_type=F32)
            scores = (scores * mask_ref[dr, h]).astype(BF16)
            qs = (q.astype(F32) * qd_ref[dr, h]).astype(BF16)
            out = jnp.dot(scores, v, preferred_element_type=F32)
            out = out + jnp.dot(qs, state.astype(BF16), preferred_element_type=F32)
            o_ref[:, sl] = out
            ks = (k.astype(F32) * kd_ref[dr, h]).astype(BF16)
            upd = lax.dot_general(ks, v, (((0,), (0,)), ((), ())), preferred_element_type=F32)
            state_ref[dr, h] = state * cd_ref[dr, h] + upd


def _retention(p, dec, li, n_lat, n_ctx, d_ret):
    t = p.shape[0]
    rc = RET_BLOCK
    assert n_lat % rc == 0 and n_ctx % rc == 0
    nb, nlb, ncb = t // rc, n_lat // rc, n_ctx // rc
    nh = d_ret // HEAD
    cq, ck, cv = 2, 3, 4

    def fmap(col):
        return lambda s: (jnp.where(s < ncb, nlb + s, s - ncb), col)

    def bmap(col):
        return lambda s: (nb - 1 - s, col)

    blk = lambda im: pl.BlockSpec((rc, d_ret), im)
    return pl.pallas_call(
        functools.partial(_ret_kernel, li=li),
        out_shape=[jax.ShapeDtypeStruct((t, d_ret), F32)] * 2,
        grid=(nb,),
        in_specs=[pl.BlockSpec(memory_space=pltpu.SMEM),
                  blk(fmap(cq)), blk(fmap(ck)), blk(fmap(cv)),
                  blk(bmap(cq)), blk(bmap(ck)), blk(bmap(cv))],
        out_specs=[blk(fmap(0)), blk(bmap(0))],
        scratch_shapes=[
            pltpu.VMEM((2, nh, HEAD, HEAD), F32),
            pltpu.VMEM((2, nh, rc, rc), F32),
            pltpu.VMEM((2, nh, rc, HEAD), F32),
            pltpu.VMEM((2, nh, rc, HEAD), F32),
            pltpu.VMEM((2, nh, HEAD, HEAD), F32),
        ],
        compiler_params=_cparams(("arbitrary",), [((rc, d_ret), BF16)] * 6 + [((rc, d_ret), F32)] * 2,
                                 [((2, nh, rc, rc), F32), ((2, nh, rc, HEAD), F32), ((2, nh, rc, HEAD), F32)]),
        name="retention",
    )(dec, p, p, p, p, p, p)


N_BUF = 4
SM_LAG = 1
PV_LAG = 3
KEY_CHUNK = 1024
EDGE_KEY_CHUNK = 256


def _key_chunks(n_k):
    if n_k <= KEY_CHUNK:
        return [(0, n_k)]
    sizes, rem = [EDGE_KEY_CHUNK], n_k - 2 * EDGE_KEY_CHUNK
    while rem > 0:
        sizes.append(min(KEY_CHUNK, rem))
        rem -= sizes[-1]
    sizes.append(EDGE_KEY_CHUNK)
    starts = [sum(sizes[:i]) for i in range(len(sizes))]
    return list(zip(starts, sizes))


def _attn_kernel(q_ref, k_ref, v_ref, lam_ref, g_ref, o_ref, *scratch, lam_init):
    tq = q_ref.shape[0]
    chunks = _key_chunks(k_ref.shape[0])
    nkv = len(chunks)
    s_bufs, p_bufs, a_bufs, pm_bufs = (scratch[i * N_BUF:(i + 1) * N_BUF] for i in range(4))
    acc_ref, m_ref, qs_ref = scratch[4 * N_BUF:]
    q = q_ref[...]
    lane = lax.broadcasted_iota(jnp.int32, q.shape, 1)
    zero = jnp.zeros_like(q)
    qs_ref[:tq] = jnp.where(lane < HEAD // 2, q, zero)
    qs_ref[tq:] = jnp.where(lane >= HEAD // 2, q, zero)
    acc_ref[...] = jnp.zeros_like(acc_ref)
    m_ref[...] = jnp.full(m_ref.shape, -jnp.inf, F32)

    def scores(c):
        r0, size = chunks[c]
        s = lax.dot_general(qs_ref[...], k_ref[r0:r0 + size, :], (((1,), (1,)), ((), ())),
                            preferred_element_type=F32)
        s_bufs[c % N_BUF][:, :size] = s
        pm = s[:, :HEAD]
        for j in range(1, size // HEAD):
            pm = jnp.maximum(pm, s[:, j * HEAD:(j + 1) * HEAD])
        pm_bufs[c % N_BUF][...] = pm

    def softmax(c):
        size = chunks[c][1]
        s_ref, p_ref = s_bufs[c % N_BUF], p_bufs[c % N_BUF]
        m_old = m_ref[...]
        m_new = jnp.maximum(m_old, jnp.broadcast_to(pm_bufs[c % N_BUF][...].max(axis=1, keepdims=True),
                                                    m_old.shape))
        a_bufs[c % N_BUF][...] = jnp.exp2(m_old - m_new)
        m_ref[...] = m_new
        for j in range(size // HEAD):
            sl = slice(j * HEAD, (j + 1) * HEAD)
            p_ref[:, sl] = jnp.exp2(s_ref[:, sl] - m_new).astype(BF16)

    def weighted_values(c):
        r0, size = chunks[c]
        v_ext = jnp.concatenate([v_ref[r0:r0 + size, :], jnp.ones((size, HEAD), BF16)], axis=1)
        pv = jnp.dot(p_bufs[c % N_BUF][:, :size], v_ext, preferred_element_type=F32)
        alpha = a_bufs[c % N_BUF][...]
        acc_ref[:, :HEAD] = alpha * acc_ref[:, :HEAD] + pv[:, :HEAD]
        acc_ref[:, HEAD:] = alpha * acc_ref[:, HEAD:] + pv[:, HEAD:]

    for t in range(nkv + PV_LAG):
        if t < nkv:
            scores(t)
        if 0 <= t - SM_LAG < nkv:
            softmax(t - SM_LAG)
        if 0 <= t - PV_LAG < nkv:
            weighted_values(t - PV_LAG)

    lv = lam_ref[...]
    lam = (jnp.exp(jnp.sum(lv[0:1] * lv[1:2], axis=1, keepdims=True))
           - jnp.exp(jnp.sum(lv[2:3] * lv[3:4], axis=1, keepdims=True)) + lam_init)
    on = acc_ref[:, :HEAD] / acc_ref[:, HEAD:]
    o = on[:tq] - lam * on[tq:]
    y = o * lax.rsqrt(jnp.mean(o * o, axis=-1, keepdims=True) + EPS) * g_ref[...]
    o_ref[...] = (y * (1.0 - lam_init)).astype(BF16)


def _attention(p, lam_vecs, subln_g, li, q_row0, n_q, k_row0, n_k, d_model, lam_init):
    d_diff = d_model // 2
    nh = d_diff // HEAD
    quarter = (d_model // 4) // HEAD
    cq, ck, cv = 6 * quarter, 8 * quarter, 10 * quarter
    tq = _tile(n_q, 512)
    tk = max(size for _, size in _key_chunks(n_k))
    assert n_k % HEAD == 0
    assert q_row0 % tq == 0 and k_row0 % n_k == 0
    qb, kb = q_row0 // tq, k_row0 // n_k
    return pl.pallas_call(
        functools.partial(_attn_kernel, lam_init=lam_init),
        out_shape=jax.ShapeDtypeStruct((n_q, d_diff), BF16),
        grid=(nh, n_q // tq),
        in_specs=[
            pl.BlockSpec((tq, HEAD), lambda h, i: (qb + i, cq + h)),
            pl.BlockSpec((n_k, HEAD), lambda h, i: (kb, ck + h)),
            pl.BlockSpec((n_k, HEAD), lambda h, i: (kb, cv + h)),
            pl.BlockSpec((None,) + lam_vecs.shape[1:], lambda h, i: (li, 0, 0)),
            pl.BlockSpec((None, 1, HEAD), lambda h, i: (li, 0, 0)),
        ],
        out_specs=pl.BlockSpec((tq, HEAD), lambda h, i: (i, h)),
        scratch_shapes=([pltpu.VMEM((2 * tq, tk), F32)] * N_BUF + [pltpu.VMEM((2 * tq, tk), BF16)] * N_BUF
                        + [pltpu.VMEM((2 * tq, HEAD), F32)] * (2 * N_BUF)
                        + [pltpu.VMEM((2 * tq, 2 * HEAD), F32), pltpu.VMEM((2 * tq, HEAD), F32),
                           pltpu.VMEM((2 * tq, HEAD), BF16)]),
        compiler_params=_cparams(("parallel", "parallel"), [((n_k, HEAD), BF16)] * 2 + [((tq, HEAD), BF16)] * 2,
                                 [((2 * tq, tk), F32)] * N_BUF + [((2 * tq, tk), BF16)] * N_BUF
                                 + [((2 * tq, HEAD), F32)] * (2 * N_BUF + 4)),
        name="diffattn",
    )(p, p, p, lam_vecs, subln_g)


def _outproj_kernel(u_ref, v_ref, gate_ref, of_ref, ob_ref, ws_ref, bs_ref, yc_ref, wa_ref, wb_ref, wc_ref,
                    x_ref, g1_ref, o_ref, ya_ref, yb_ref):
    tm = x_ref.shape[0]
    for n in range(tm // SGU_CHUNK):
        rs = slice(n * SGU_CHUNK, (n + 1) * SGU_CHUNK)
        for g in range(u_ref.shape[1] // HEAD):
            cs = slice(g * HEAD, (g + 1) * HEAD)
            mixed = jnp.dot(ws_ref[g], v_ref[rs, cs], preferred_element_type=F32) + bs_ref[g]
            ya_ref[rs, cs] = (u_ref[rs, cs].astype(F32) * mixed).astype(BF16)
    for h in range(of_ref.shape[1] // HEAD):
        cs = slice(h * HEAD, (h + 1) * HEAD)
        o = of_ref[:, cs] + ob_ref[:, cs]
        yb_ref[:, cs] = (_group_rms(o) * gate_ref[:, cs].astype(F32)).astype(BF16)
    acc = jnp.dot(ya_ref[...], wa_ref[...], preferred_element_type=F32)
    acc = acc + jnp.dot(yb_ref[...], wb_ref[...], preferred_element_type=F32)
    acc = acc + jnp.dot(yc_ref[...], wc_ref[...], preferred_element_type=F32)
    o_ref[...] = x_ref[...] + g1_ref[...] * acc


def _outproj(p, o_f, o_b, ws, bs, yc, y_row0, w, x, mods, li, row):
    m, d = x.shape
    dq = d // 4
    tm = _tile(m, 512)
    assert y_row0 % tm == 0 and tm % SGU_CHUNK == 0
    yb0 = y_row0 // tm
    tok = lambda col: pl.BlockSpec((tm, dq), lambda i: (yb0 + i, col))
    layer = lambda a: pl.BlockSpec((None,) + a.shape[1:], lambda i: (li,) + (0,) * (a.ndim - 1))
    return pl.pallas_call(
        _outproj_kernel,
        out_shape=jax.ShapeDtypeStruct((m, d), F32),
        grid=(m // tm,),
        in_specs=[
            tok(0), tok(1), tok(5), tok(0), tok(0), layer(ws), layer(bs),
            pl.BlockSpec((tm, 2 * dq), lambda i: (i, 0)),
            pl.BlockSpec((None, dq, d), lambda i: (li, 0, 0)),
            pl.BlockSpec((None, dq, d), lambda i: (li, 1, 0)),
            pl.BlockSpec((None, 2 * dq, d), lambda i: (li, 1, 0)),
            pl.BlockSpec((tm, d), lambda i: (i, 0)),
            pl.BlockSpec((None, None, None, 1, d), lambda i: (li, row, 2, 0, 0)),
        ],
        out_specs=pl.BlockSpec((tm, d), lambda i: (i, 0)),
        scratch_shapes=[pltpu.VMEM((tm, dq), BF16), pltpu.VMEM((tm, dq), BF16)],
        compiler_params=_cparams(("parallel",), [((tm, d), BF16), ((d, d), BF16), ((tm, d), F32), ((tm, d), F32),
                                                 ((tm, dq), F32), ((tm, dq), F32)], [((tm, d), BF16)]),
        name="outproj",
    )(p, p, p, o_f, o_b, ws, bs, yc, w, w, w, x, mods)


def _ffn_kernel(x_ref, sh_ref, sc_ref, gate_ref, g_ref, wg_ref, wu_ref, wd_ref, fg_ref, o_ref, h_ref,
                *, final_norm, n_ff_steps):
    f = pl.program_id(1)

    nf = pl.num_programs(1)

    def swiglu_slice(first, last):
        h = h_ref[...]
        a = jnp.dot(h, wg_ref[...].astype(BF16), preferred_element_type=F32)
        b = jnp.dot(h, wu_ref[...].astype(BF16), preferred_element_type=F32)
        act = (a * jax.nn.sigmoid(a) * b).astype(BF16)
        acc = jnp.dot(act, wd_ref[...].astype(BF16), preferred_element_type=F32)
        if not first:
            acc = o_ref[...] + acc
        if last:
            acc = x_ref[...] + gate_ref[...] * acc
            if final_norm:
                acc = acc * lax.rsqrt(jnp.mean(acc * acc, axis=-1, keepdims=True) + EPS) * fg_ref[...]
        o_ref[...] = acc

    @pl.when(f == 0)
    def _():
        h_ref[...] = _modulated_norm(x_ref[...], g_ref[...], sh_ref[...], sc_ref[...]).astype(BF16)
        swiglu_slice(True, n_ff_steps == 1)

    if n_ff_steps > 2:
        @pl.when((f != 0) & (f != nf - 1))
        def _():
            swiglu_slice(False, False)

    if n_ff_steps > 1:
        @pl.when(f == nf - 1)
        def _():
            swiglu_slice(False, True)


def _ffn(x, mods, li, row, g, wg, wu, wd, final_g, final_norm):
    m, d = x.shape
    ff = wg.shape[2]
    tm = _tile(m, 1024)
    tf = _tile(ff, 256)
    mod = lambda c: pl.BlockSpec((None, None, None, 1, d), lambda i, f: (li, row, c, 0, 0))
    return pl.pallas_call(
        functools.partial(_ffn_kernel, final_norm=final_norm, n_ff_steps=ff // tf),
        out_shape=jax.ShapeDtypeStruct((m, d), F32),
        grid=(m // tm, ff // tf),
        in_specs=[
            pl.BlockSpec((tm, d), lambda i, f: (i, 0), pipeline_mode=pl.Buffered(1)),
            mod(3), mod(4), mod(5),
            pl.BlockSpec((None, 1, d), lambda i, f: (li, 0, 0)),
            pl.BlockSpec((None, d, tf), lambda i, f: (li, 0, f)),
            pl.BlockSpec((None, d, tf), lambda i, f: (li, 0, f)),
            pl.BlockSpec((None, tf, d), lambda i, f: (li, f, 0)),
            pl.BlockSpec((1, d), lambda i, f: (0, 0)),
        ],
        out_specs=pl.BlockSpec((tm, d), lambda i, f: (i, 0)),
        scratch_shapes=[pltpu.VMEM((tm, d), BF16)],
        compiler_params=_cparams(("parallel", "arbitrary"),
                                 [((tm, d), F32), ((tm, d), F32)] + [((d, tf), wg.dtype)] * 3, [((tm, d), BF16)]),
        name="ffn",
    )(x, mods, mods, mods, g, wg, wu, wd, final_g)


def _rope_tables(n_rows, identity):
    def one(width):
        half = width // 2
        lane = np.arange(HEAD)
        sub = lane // width
        idx = lane % width
        first = idx < half
        freqs = np.float32(ROPE_BASE) ** (-(idx % half).astype(np.float32) / np.float32(half))
        t = np.arange(n_rows)
        pos = np.where((sub % 2) == 0, (t // GRID_W)[:, None], (t % GRID_W)[:, None]).astype(np.float32)
        ang = pos * freqs[None, :]
        cos, sin = np.cos(ang), np.sin(ang)
        if identity:
            cos, sin = np.ones_like(cos), np.zeros_like(sin)
        return [cos, np.where(first[None, :], -sin, 0.0), np.where(first[None, :], 0.0, sin)]
    return jnp.asarray(np.stack(one(HEAD // 2) + one(HEAD // 4)).astype(np.float32))


def kernel(x, c, ctx, c_ctx, w_ada, b_ada, norm1_g, w_in, sgu_w, sgu_b, ret_decay_fwd, ret_decay_bwd,
           diff_lambda_q1, diff_lambda_k1, diff_lambda_q2, diff_lambda_k2, diff_subln_g, w_out, norm2_g,
           w_gate, w_up, w_down, final_g):
    assert x.shape[0] == 1 and ctx.shape[0] == 1
    depth = w_ada.shape[0]
    xl, xc = x[0], ctx[0]
    n_lat, d = xl.shape
    n_ctx = xc.shape[0]
    total = n_lat + n_ctx
    dq = d // 4

    mods = _ada(jnp.concatenate([c, c_ctx[None, :]], axis=0).T, w_ada, b_ada)
    mods = mods.reshape(depth, 2, N_MOD, 1, d)
    tabs_lat = _rope_tables(n_lat, identity=False)
    tabs_ctx = _rope_tables(n_ctx, identity=True)
    w_out_b = w_out.astype(BF16)
    sgu_w_b = sgu_w.astype(BF16)
    sgu_b_col = sgu_b[..., None]
    g1, g2, fg = norm1_g[:, None, :], norm2_g[:, None, :], final_g[None, :]
    dec = jnp.stack([ret_decay_fwd, ret_decay_bwd]).astype(F32)
    lam_vecs = jnp.stack([diff_lambda_q1, diff_lambda_k1, diff_lambda_q2, diff_lambda_k2], axis=1).astype(F32)
    sub_g = diff_subln_g[:, None, :]

    for li in range(depth):
        need_ctx = li < depth - 1
        last = li == depth - 1
        lam_init = 0.8 - 0.6 * math.exp(-0.3 * li)
        p = _inproj(xl, mods, li, 0, g1, w_in, tabs_lat, total, 0)
        p = _inproj(xc, mods, li, 1, g1, w_in, tabs_ctx, total, n_lat, p_prev=p)
        o_f, o_b = _retention(p, dec, li, n_lat, n_ctx, dq)
        y_c = _attention(p, lam_vecs, sub_g, li, 0, n_lat, 0, total, d, lam_init)
        xl1 = _outproj(p, o_f, o_b, sgu_w_b, sgu_b_col, y_c, 0, w_out_b, xl, mods, li, 0)
        xl_new = _ffn(xl1, mods, li, 0, g2, w_gate, w_up, w_down, fg, last)
        if need_ctx:
            yc_c = _attention(p, lam_vecs, sub_g, li, n_lat, n_ctx, n_lat, n_ctx, d, lam_init)
            xc1 = _outproj(p, o_f, o_b, sgu_w_b, sgu_b_col, yc_c, n_lat, w_out_b, xc, mods, li, 1)
            xc = _ffn(xc1, mods, li, 1, g2, w_gate, w_up, w_down, fg, False)
        xl = xl_new
    return xl[None]
```

```python
import functools
import math

import jax
import jax.numpy as jnp
import numpy as np
from jax import lax
from jax.experimental import pallas as pl
from jax.experimental.pallas import tpu as pltpu

GRID_W = 64
ROPE_BASE = 10000.0
EPS = 1e-6
N_MOD = 6
HEAD = 128
SGU_CHUNK = 128
RET_BLOCK = 256
IN_TILES = 6
ROW_CHUNK = 256
V7X_VMEM_BYTES = 64 * 1024 * 1024
VMEM_RESERVE_BYTES = 2 * 1024 * 1024

F32 = jnp.float32
BF16 = jnp.bfloat16


def _nbytes(shape, dtype):
    return math.prod(shape) * jnp.dtype(dtype).itemsize


def _cparams(sem, windows, scratch=()):
    est = 2 * sum(_nbytes(*w) for w in windows) + sum(_nbytes(*b) for b in scratch)
    limit = min(V7X_VMEM_BYTES - VMEM_RESERVE_BYTES, 2 * est + 16 * 1024 * 1024)
    return pltpu.CompilerParams(dimension_semantics=sem, vmem_limit_bytes=limit)


def _tile(n, want):
    if n <= want:
        return n
    t = want
    while n % t:
        t -= 8
    return t


def _ada_kernel(a_ref, w_ref, b_ref, o_ref, ab_ref, *, rows):
    d, tn = w_ref.shape[1], w_ref.shape[2]

    @pl.when((pl.program_id(0) == 0) & (pl.program_id(1) == 0))
    def _():
        a = a_ref[...]
        a = a * jax.nn.sigmoid(a)
        ab_ref[0] = jnp.broadcast_to(a[:, 0:1], (d, HEAD))
        ab_ref[1] = jnp.broadcast_to(a[:, 1:2], (d, HEAD))

    def body(r, acc):
        r0 = pl.multiple_of(r * rows, rows)
        a0 = ab_ref[0, pl.ds(r0, rows), :]
        a1 = ab_ref[1, pl.ds(r0, rows), :]
        new = []
        for c in range(tn // HEAD):
            w = w_ref[0, pl.ds(r0, rows), c * HEAD:(c + 1) * HEAD]
            new.append(acc[2 * c] + (w * a0).reshape(rows // 8, 8, HEAD).sum(axis=0))
            new.append(acc[2 * c + 1] + (w * a1).reshape(rows // 8, 8, HEAD).sum(axis=0))
        return tuple(new)

    z = jnp.zeros((8, HEAD), F32)
    acc = lax.fori_loop(0, d // rows, body, (z,) * (2 * (tn // HEAD)))
    for c in range(tn // HEAD):
        out = jnp.concatenate([acc[2 * c].sum(axis=0, keepdims=True), acc[2 * c + 1].sum(axis=0, keepdims=True)],
                              axis=0)
        o_ref[0, :, c * HEAD:(c + 1) * HEAD] = out + b_ref[0, :, c * HEAD:(c + 1) * HEAD]


def _ada(c2t, w_ada, b_ada):
    depth, d, n = w_ada.shape
    tn = _tile(n, 1024)
    return pl.pallas_call(
        functools.partial(_ada_kernel, rows=128),
        out_shape=jax.ShapeDtypeStruct((depth, 2, n), F32),
        grid=(depth, n // tn),
        in_specs=[
            pl.BlockSpec((d, 2), lambda l, j: (0, 0)),
            pl.BlockSpec((1, d, tn), lambda l, j: (l, 0, j)),
            pl.BlockSpec((1, 1, tn), lambda l, j: (l, 0, j)),
        ],
        out_specs=pl.BlockSpec((1, 2, tn), lambda l, j: (l, 0, j)),
        scratch_shapes=[pltpu.VMEM((2, d, HEAD), F32)],
        compiler_params=_cparams(("arbitrary", "arbitrary"), [((d, tn), F32), ((d, HEAD), F32)],
                                 [((2, d, HEAD), F32)]),
        name="ada",
    )(c2t, w_ada, b_ada.reshape(depth, 1, n))


def _rope(x, c, s1, s2, shift):
    return x * c + pltpu.roll(x, HEAD - shift, 1) * s1 + pltpu.roll(x, shift, 1) * s2


def _gelu_tanh(x):
    c = 2.0 * math.sqrt(2.0 / math.pi)
    return x * jax.nn.sigmoid(x * (c + (c * 0.044715) * (x * x)))


def _group_rms(x):
    return x * lax.rsqrt(jnp.mean(x * x, axis=-1, keepdims=True) + EPS)


def _modulated_norm(x, g, sh, sc):
    y = x * lax.rsqrt(jnp.mean(x * x, axis=-1, keepdims=True) + EPS) * g
    return y * (1.0 + sc) + sh


def _inproj_kernel(x_ref, sh_ref, sc_ref, g_ref, w_ref, tab_ref, *rest, aliased):
    o_ref, h_ref = rest[1:] if aliased else rest
    j = pl.program_id(1)
    tn = w_ref.shape[1]

    @pl.when(j == 0)
    def _():
        h_ref[...] = _modulated_norm(x_ref[...], g_ref[...], sh_ref[...], sc_ref[...]).astype(BF16)

    def project(*fns):
        rc = min(h_ref.shape[0], ROW_CHUNK)
        part = tn // len(fns)
        slab = min(2 * HEAD, part)
        for c0 in range(0, tn, slab):
            fn = fns[c0 // part]
            w = w_ref[:, c0:c0 + slab].astype(BF16)
            for r in range(h_ref.shape[0] // rc):
                rs = slice(r * rc, (r + 1) * rc)
                acc = jnp.dot(h_ref[rs, :], w, preferred_element_type=F32)
                for hh in range(slab // HEAD):
                    o_ref[rs, c0 + hh * HEAD:c0 + (hh + 1) * HEAD] = fn(
                        acc[:, hh * HEAD:(hh + 1) * HEAD], rs).astype(BF16)

    def rope_ret(scale):
        return lambda a, rs: _rope(a, tab_ref[0, rs, :], tab_ref[1, rs, :], tab_ref[2, rs, :], HEAD // 4) * scale

    def rope_diff(scale):
        return lambda a, rs: _rope(a, tab_ref[3, rs, :], tab_ref[4, rs, :], tab_ref[5, rs, :], HEAD // 8) * scale

    @pl.when(j == 0)
    def _():
        project(lambda a, rs: _gelu_tanh(a), lambda a, rs: _group_rms(_gelu_tanh(a)))

    @pl.when(j == 1)
    def _():
        project(rope_ret(1.0), rope_ret(HEAD ** -0.5))

    @pl.when(j == 2)
    def _():
        project(lambda a, rs: a, lambda a, rs: a * jax.nn.sigmoid(a))

    @pl.when(j == 3)
    def _():
        project(rope_diff((HEAD // 2) ** -0.5 * math.log2(math.e)))

    @pl.when(j == 4)
    def _():
        project(rope_diff(1.0))

    @pl.when(j == 5)
    def _():
        project(lambda a, rs: a)


def _inproj(x, mods, li, row, g, w, tabs, total_rows, row_off, p_prev=None):
    m, d = x.shape
    n = w.shape[2]
    tn = n // IN_TILES
    tm = _tile(m, 1024)
    assert row_off % tm == 0
    ob = row_off // tm
    aliased = p_prev is not None
    in_specs = [
        pl.BlockSpec((tm, d), lambda i, j: (i, 0)),
        pl.BlockSpec((None, None, None, 1, d), lambda i, j: (li, row, 0, 0, 0)),
        pl.BlockSpec((None, None, None, 1, d), lambda i, j: (li, row, 1, 0, 0)),
        pl.BlockSpec((None, 1, d), lambda i, j: (li, 0, 0)),
        pl.BlockSpec((None, d, tn), lambda i, j: (li, 0, j)),
        pl.BlockSpec((6, tm, HEAD), lambda i, j: (0, i, 0)),
    ]
    args = [x, mods, mods, g, w, tabs]
    if aliased:
        in_specs.append(pl.BlockSpec(memory_space=pl.ANY))
        args.append(p_prev)
    return pl.pallas_call(
        functools.partial(_inproj_kernel, aliased=aliased),
        out_shape=jax.ShapeDtypeStruct((total_rows, n), BF16),
        grid=(m // tm, IN_TILES),
        in_specs=in_specs,
        out_specs=pl.BlockSpec((tm, tn), lambda i, j: (ob + i, j)),
        scratch_shapes=[pltpu.VMEM((tm, d), BF16)],
        input_output_aliases={6: 0} if aliased else {},
        compiler_params=_cparams(("parallel", "arbitrary"),
                                 [((tm, d), x.dtype), ((d, tn), w.dtype), ((6, tm, HEAD), F32), ((tm, tn), BF16)],
                                 [((tm, d), BF16)]),
        name="inproj",
    )(*args)


def _log_sigmoid(z):
    return jnp.minimum(z, 0.0) - jnp.log1p(jnp.exp(-jnp.abs(z)))


def _ret_kernel(dec_ref, qf_ref, kf_ref, vf_ref, qb_ref, kb_ref, vb_ref, of_ref, ob_ref,
                state_ref, mask_ref, qd_ref, kd_ref, cd_ref, *, li):
    s = pl.program_id(0)
    rc = qf_ref.shape[0]
    nh = qf_ref.shape[1] // HEAD

    @pl.when(s == 0)
    def _():
        state_ref[...] = jnp.zeros_like(state_ref)
        ii = lax.broadcasted_iota(jnp.int32, (rc, rc), 0).astype(F32)
        jj = lax.broadcasted_iota(jnp.int32, (rc, rc), 1).astype(F32)
        col = lax.broadcasted_iota(jnp.int32, (rc, HEAD), 0).astype(F32)
        for dr in range(2):
            rel = (ii - jj) if dr == 0 else (jj - ii)
            for h in range(nh):
                logit = dec_ref[dr, li, h]
                lg = _log_sigmoid(jnp.full((rc, rc), logit, F32))
                mask_ref[dr, h] = jnp.where(rel >= 0, jnp.exp(jnp.maximum(rel, 0.0) * lg), 0.0)
                lgc = _log_sigmoid(jnp.full((rc, HEAD), logit, F32))
                q_pow = (col + 1.0) if dr == 0 else (rc - col)
                k_pow = (rc - 1.0 - col) if dr == 0 else col
                qd_ref[dr, h] = jnp.exp(q_pow * lgc)
                kd_ref[dr, h] = jnp.exp(k_pow * lgc)
                cd_ref[dr, h] = jnp.exp(rc * _log_sigmoid(jnp.full((HEAD, HEAD), logit, F32)))

    for dr, (q_ref, k_ref, v_ref, o_ref) in enumerate(
            ((qf_ref, kf_ref, vf_ref, of_ref), (qb_ref, kb_ref, vb_ref, ob_ref))):
        for h in range(nh):
            sl = slice(h * HEAD, (h + 1) * HEAD)
            q = q_ref[:, sl]
            k = k_ref[:, sl]
            v = v_ref[:, sl]
            state = state_ref[dr, h]
            scores = lax.dot_general(q, k, (((1,), (1,)), ((), ())), preferred_element_type=F32)
            scores = (scores * mask_ref[dr, h]).astype(BF16)
            qs = (q.astype(F32) * qd_ref[dr, h]).astype(BF16)
            out = jnp.dot(scores, v, preferred_element_type=F32)
            out = out + jnp.dot(qs, state.astype(BF16), preferred_element_type=F32)
            o_ref[:, sl] = out
            ks = (k.astype(F32) * kd_ref[dr, h]).astype(BF16)
            upd = lax.dot_general(ks, v, (((0,), (0,)), ((), ())), preferred_element_type=F32)
            state_ref[dr, h] = state * cd_ref[dr, h] + upd


def _retention(p, dec, li, n_lat, n_ctx, d_ret):
    t = p.shape[0]
    rc = RET_BLOCK
    assert n_lat % rc == 0 and n_ctx % rc == 0
    nb, nlb, ncb = t // rc, n_lat // rc, n_ctx // rc
    nh = d_ret // HEAD
    cq, ck, cv = 2, 3, 4

    def fmap(col):
        return lambda s: (jnp.where(s < ncb, nlb + s, s - ncb), col)

    def bmap(col):
        return lambda s: (nb - 1 - s, col)

    blk = lambda im: pl.BlockSpec((rc, d_ret), im)
    return pl.pallas_call(
        functools.partial(_ret_kernel, li=li),
        out_shape=[jax.ShapeDtypeStruct((t, d_ret), F32)] * 2,
        grid=(nb,),
        in_specs=[pl.BlockSpec(memory_space=pltpu.SMEM),
                  blk(fmap(cq)), blk(fmap(ck)), blk(fmap(cv)),
                  blk(bmap(cq)), blk(bmap(ck)), blk(bmap(cv))],
        out_specs=[blk(fmap(0)), blk(bmap(0))],
        scratch_shapes=[
            pltpu.VMEM((2, nh, HEAD, HEAD), F32),
            pltpu.VMEM((2, nh, rc, rc), F32),
            pltpu.VMEM((2, nh, rc, HEAD), F32),
            pltpu.VMEM((2, nh, rc, HEAD), F32),
            pltpu.VMEM((2, nh, HEAD, HEAD), F32),
        ],
        compiler_params=_cparams(("arbitrary",), [((rc, d_ret), BF16)] * 6 + [((rc, d_ret), F32)] * 2,
                                 [((2, nh, rc, rc), F32), ((2, nh, rc, HEAD), F32), ((2, nh, rc, HEAD), F32)]),
        name="retention",
    )(dec, p, p, p, p, p, p)


N_BUF = 4
SM_LAG = 1
PV_LAG = 3
KEY_CHUNK = 2048
FIRST_KEY_CHUNK = 256


def _key_chunks(n_k):
    if n_k <= KEY_CHUNK:
        return [(0, n_k)]
    chunks, r0 = [(0, FIRST_KEY_CHUNK)], FIRST_KEY_CHUNK
    while r0 < n_k:
        size = min(KEY_CHUNK, n_k - r0)
        chunks.append((r0, size))
        r0 += size
    return chunks


def _attn_kernel(q_ref, k_ref, v_ref, lam_ref, g_ref, o_ref, *scratch, lam_init):
    tq = q_ref.shape[0]
    chunks = _key_chunks(k_ref.shape[0])
    nkv = len(chunks)
    s_bufs, p_bufs, a_bufs, pm_bufs = (scratch[i * N_BUF:(i + 1) * N_BUF] for i in range(4))
    acc_ref, m_ref, qs_ref = scratch[4 * N_BUF:]
    q = q_ref[...]
    lane = lax.broadcasted_iota(jnp.int32, q.shape, 1)
    zero = jnp.zeros_like(q)
    qs_ref[:tq] = jnp.where(lane < HEAD // 2, q, zero)
    qs_ref[tq:] = jnp.where(lane >= HEAD // 2, q, zero)
    acc_ref[...] = jnp.zeros_like(acc_ref)
    m_ref[...] = jnp.full(m_ref.shape, -jnp.inf, F32)

    def scores(c):
        r0, size = chunks[c]
        s = lax.dot_general(qs_ref[...], k_ref[r0:r0 + size, :], (((1,), (1,)), ((), ())),
                            preferred_element_type=F32)
        s_bufs[c % N_BUF][:, :size] = s
        pm = s[:, :HEAD]
        for j in range(1, size // HEAD):
            pm = jnp.maximum(pm, s[:, j * HEAD:(j + 1) * HEAD])
        pm_bufs[c % N_BUF][...] = pm

    def softmax(c):
        size = chunks[c][1]
        s_ref, p_ref = s_bufs[c % N_BUF], p_bufs[c % N_BUF]
        m_old = m_ref[...]
        m_new = jnp.maximum(m_old, jnp.broadcast_to(pm_bufs[c % N_BUF][...].max(axis=1, keepdims=True),
                                                    m_old.shape))
        a_bufs[c % N_BUF][...] = jnp.exp2(m_old - m_new)
        m_ref[...] = m_new
        for j in range(size // HEAD):
            sl = slice(j * HEAD, (j + 1) * HEAD)
            p_ref[:, sl] = jnp.exp2(s_ref[:, sl] - m_new).astype(BF16)

    def weighted_values(c):
        r0, size = chunks[c]
        v_ext = jnp.concatenate([v_ref[r0:r0 + size, :], jnp.ones((size, HEAD), BF16)], axis=1)
        pv = jnp.dot(p_bufs[c % N_BUF][:, :size], v_ext, preferred_element_type=F32)
        alpha = a_bufs[c % N_BUF][...]
        acc_ref[:, :HEAD] = alpha * acc_ref[:, :HEAD] + pv[:, :HEAD]
        acc_ref[:, HEAD:] = alpha * acc_ref[:, HEAD:] + pv[:, HEAD:]

    for t in range(nkv + PV_LAG):
        if t < nkv:
            scores(t)
        if 0 <= t - SM_LAG < nkv:
            softmax(t - SM_LAG)
        if 0 <= t - PV_LAG < nkv:
            weighted_values(t - PV_LAG)

    lv = lam_ref[...]
    lam = (jnp.exp(jnp.sum(lv[0:1] * lv[1:2], axis=1, keepdims=True))
           - jnp.exp(jnp.sum(lv[2:3] * lv[3:4], axis=1, keepdims=True)) + lam_init)
    on = acc_ref[:, :HEAD] / acc_ref[:, HEAD:]
    o = on[:tq] - lam * on[tq:]
    y = o * lax.rsqrt(jnp.mean(o * o, axis=-1, keepdims=True) + EPS) * g_ref[...]
    o_ref[...] = (y * (1.0 - lam_init)).astype(BF16)


def _attention(p, lam_vecs, subln_g, li, q_row0, n_q, k_row0, n_k, d_model, lam_init):
    d_diff = d_model // 2
    nh = d_diff // HEAD
    quarter = (d_model // 4) // HEAD
    cq, ck, cv = 6 * quarter, 8 * quarter, 10 * quarter
    tq = _tile(n_q, 512)
    tk = max(size for _, size in _key_chunks(n_k))
    assert n_k % HEAD == 0
    assert q_row0 % tq == 0 and k_row0 % n_k == 0
    qb, kb = q_row0 // tq, k_row0 // n_k
    return pl.pallas_call(
        functools.partial(_attn_kernel, lam_init=lam_init),
        out_shape=jax.ShapeDtypeStruct((n_q, d_diff), BF16),
        grid=(nh, n_q // tq),
        in_specs=[
            pl.BlockSpec((tq, HEAD), lambda h, i: (qb + i, cq + h)),
            pl.BlockSpec((n_k, HEAD), lambda h, i: (kb, ck + h)),
            pl.BlockSpec((n_k, HEAD), lambda h, i: (kb, cv + h)),
            pl.BlockSpec((None,) + lam_vecs.shape[1:], lambda h, i: (li, 0, 0)),
            pl.BlockSpec((None, 1, HEAD), lambda h, i: (li, 0, 0)),
        ],
        out_specs=pl.BlockSpec((tq, HEAD), lambda h, i: (i, h)),
        scratch_shapes=([pltpu.VMEM((2 * tq, tk), F32)] * N_BUF + [pltpu.VMEM((2 * tq, tk), BF16)] * N_BUF
                        + [pltpu.VMEM((2 * tq, HEAD), F32)] * (2 * N_BUF)
                        + [pltpu.VMEM((2 * tq, 2 * HEAD), F32), pltpu.VMEM((2 * tq, HEAD), F32),
                           pltpu.VMEM((2 * tq, HEAD), BF16)]),
        compiler_params=_cparams(("parallel", "parallel"), [((n_k, HEAD), BF16)] * 2 + [((tq, HEAD), BF16)] * 2,
                                 [((2 * tq, tk), F32)] * N_BUF + [((2 * tq, tk), BF16)] * N_BUF
                                 + [((2 * tq, HEAD), F32)] * (2 * N_BUF + 4)),
        name="diffattn",
    )(p, p, p, lam_vecs, subln_g)


def _outproj_kernel(u_ref, v_ref, gate_ref, of_ref, ob_ref, ws_ref, bs_ref, yc_ref, wa_ref, wb_ref, wc_ref,
                    x_ref, g1_ref, o_ref, ya_ref, yb_ref):
    tm = x_ref.shape[0]
    for n in range(tm // SGU_CHUNK):
        rs = slice(n * SGU_CHUNK, (n + 1) * SGU_CHUNK)
        for g in range(u_ref.shape[1] // HEAD):
            cs = slice(g * HEAD, (g + 1) * HEAD)
            mixed = jnp.dot(ws_ref[g], v_ref[rs, cs], preferred_element_type=F32) + bs_ref[g]
            ya_ref[rs, cs] = (u_ref[rs, cs].astype(F32) * mixed).astype(BF16)
    for h in range(of_ref.shape[1] // HEAD):
        cs = slice(h * HEAD, (h + 1) * HEAD)
        o = of_ref[:, cs] + ob_ref[:, cs]
        yb_ref[:, cs] = (_group_rms(o) * gate_ref[:, cs].astype(F32)).astype(BF16)
    acc = jnp.dot(ya_ref[...], wa_ref[...], preferred_element_type=F32)
    acc = acc + jnp.dot(yb_ref[...], wb_ref[...], preferred_element_type=F32)
    acc = acc + jnp.dot(yc_ref[...], wc_ref[...], preferred_element_type=F32)
    o_ref[...] = x_ref[...] + g1_ref[...] * acc


def _outproj(p, o_f, o_b, ws, bs, yc, y_row0, w, x, mods, li, row):
    m, d = x.shape
    dq = d // 4
    tm = _tile(m, 512)
    assert y_row0 % tm == 0 and tm % SGU_CHUNK == 0
    yb0 = y_row0 // tm
    tok = lambda col: pl.BlockSpec((tm, dq), lambda i: (yb0 + i, col))
    layer = lambda a: pl.BlockSpec((None,) + a.shape[1:], lambda i: (li,) + (0,) * (a.ndim - 1))
    return pl.pallas_call(
        _outproj_kernel,
        out_shape=jax.ShapeDtypeStruct((m, d), F32),
        grid=(m // tm,),
        in_specs=[
            tok(0), tok(1), tok(5), tok(0), tok(0), layer(ws), layer(bs),
            pl.BlockSpec((tm, 2 * dq), lambda i: (i, 0)),
            pl.BlockSpec((None, dq, d), lambda i: (li, 0, 0)),
            pl.BlockSpec((None, dq, d), lambda i: (li, 1, 0)),
            pl.BlockSpec((None, 2 * dq, d), lambda i: (li, 1, 0)),
            pl.BlockSpec((tm, d), lambda i: (i, 0)),
            pl.BlockSpec((None, None, None, 1, d), lambda i: (li, row, 2, 0, 0)),
        ],
        out_specs=pl.BlockSpec((tm, d), lambda i: (i, 0)),
        scratch_shapes=[pltpu.VMEM((tm, dq), BF16), pltpu.VMEM((tm, dq), BF16)],
        compiler_params=_cparams(("parallel",), [((tm, d), BF16), ((d, d), BF16), ((tm, d), F32), ((tm, d), F32),
                                                 ((tm, dq), F32), ((tm, dq), F32)], [((tm, d), BF16)]),
        name="outproj",
    )(p, p, p, o_f, o_b, ws, bs, yc, w, w, w, x, mods)


def _ffn_kernel(x_ref, sh_ref, sc_ref, gate_ref, g_ref, wg_ref, wu_ref, wd_ref, fg_ref, o_ref, h_ref,
                *, final_norm, n_ff_steps):
    f = pl.program_id(1)

    nf = pl.num_programs(1)

    def swiglu_slice(first, last):
        h = h_ref[...]
        a = jnp.dot(h, wg_ref[...].astype(BF16), preferred_element_type=F32)
        b = jnp.dot(h, wu_ref[...].astype(BF16), preferred_element_type=F32)
        act = (a * jax.nn.sigmoid(a) * b).astype(BF16)
        acc = jnp.dot(act, wd_ref[...].astype(BF16), preferred_element_type=F32)
        if not first:
            acc = o_ref[...] + acc
        if last:
            acc = x_ref[...] + gate_ref[...] * acc
            if final_norm:
                acc = acc * lax.rsqrt(jnp.mean(acc * acc, axis=-1, keepdims=True) + EPS) * fg_ref[...]
        o_ref[...] = acc

    @pl.when(f == 0)
    def _():
        h_ref[...] = _modulated_norm(x_ref[...], g_ref[...], sh_ref[...], sc_ref[...]).astype(BF16)
        swiglu_slice(True, n_ff_steps == 1)

    if n_ff_steps > 2:
        @pl.when((f != 0) & (f != nf - 1))
        def _():
            swiglu_slice(False, False)

    if n_ff_steps > 1:
        @pl.when(f == nf - 1)
        def _():
            swiglu_slice(False, True)


def _ffn(x, mods, li, row, g, wg, wu, wd, final_g, final_norm):
    m, d = x.shape
    ff = wg.shape[2]
    tm = _tile(m, 1024)
    tf = _tile(ff, 256)
    mod = lambda c: pl.BlockSpec((None, None, None, 1, d), lambda i, f: (li, row, c, 0, 0))
    return pl.pallas_call(
        functools.partial(_ffn_kernel, final_norm=final_norm, n_ff_steps=ff // tf),
        out_shape=jax.ShapeDtypeStruct((m, d), F32),
        grid=(m // tm, ff // tf),
        in_specs=[
            pl.BlockSpec((tm, d), lambda i, f: (i, 0), pipeline_mode=pl.Buffered(1)),
            mod(3), mod(4), mod(5),
            pl.BlockSpec((None, 1, d), lambda i, f: (li, 0, 0)),
            pl.BlockSpec((None, d, tf), lambda i, f: (li, 0, f)),
            pl.BlockSpec((None, d, tf), lambda i, f: (li, 0, f)),
            pl.BlockSpec((None, tf, d), lambda i, f: (li, f, 0)),
            pl.BlockSpec((1, d), lambda i, f: (0, 0)),
        ],
        out_specs=pl.BlockSpec((tm, d), lambda i, f: (i, 0)),
        scratch_shapes=[pltpu.VMEM((tm, d), BF16)],
        compiler_params=_cparams(("parallel", "arbitrary"),
                                 [((tm, d), F32), ((tm, d), F32)] + [((d, tf), wg.dtype)] * 3, [((tm, d), BF16)]),
        name="ffn",
    )(x, mods, mods, mods, g, wg, wu, wd, final_g)


def _rope_tables(n_rows, identity):
    def one(width):
        half = width // 2
        lane = np.arange(HEAD)
        sub = lane // width
        idx = lane % width
        first = idx < half
        freqs = np.float32(ROPE_BASE) ** (-(idx % half).astype(np.float32) / np.float32(half))
        t = np.arange(n_rows)
        pos = np.where((sub % 2) == 0, (t // GRID_W)[:, None], (t % GRID_W)[:, None]).astype(np.float32)
        ang = pos * freqs[None, :]
        cos, sin = np.cos(ang), np.sin(ang)
        if identity:
            cos, sin = np.ones_like(cos), np.zeros_like(sin)
        return [cos, np.where(first[None, :], -sin, 0.0), np.where(first[None, :], 0.0, sin)]
    return jnp.asarray(np.stack(one(HEAD // 2) + one(HEAD // 4)).astype(np.float32))


def kernel(x, c, ctx, c_ctx, w_ada, b_ada, norm1_g, w_in, sgu_w, sgu_b, ret_decay_fwd, ret_decay_bwd,
           diff_lambda_q1, diff_lambda_k1, diff_lambda_q2, diff_lambda_k2, diff_subln_g, w_out, norm2_g,
           w_gate, w_up, w_down, final_g):
    assert x.shape[0] == 1 and ctx.shape[0] == 1
    depth = w_ada.shape[0]
    xl, xc = x[0], ctx[0]
    n_lat, d = xl.shape
    n_ctx = xc.shape[0]
    total = n_lat + n_ctx
    dq = d // 4

    mods = _ada(jnp.concatenate([c, c_ctx[None, :]], axis=0).T, w_ada, b_ada)
    mods = mods.reshape(depth, 2, N_MOD, 1, d)
    tabs_lat = _rope_tables(n_lat, identity=False)
    tabs_ctx = _rope_tables(n_ctx, identity=True)
    w_out_b = w_out.astype(BF16)
    sgu_w_b = sgu_w.astype(BF16)
    sgu_b_col = sgu_b[..., None]
    g1, g2, fg = norm1_g[:, None, :], norm2_g[:, None, :], final_g[None, :]
    dec = jnp.stack([ret_decay_fwd, ret_decay_bwd]).astype(F32)
    lam_vecs = jnp.stack([diff_lambda_q1, diff_lambda_k1, diff_lambda_q2, diff_lambda_k2], axis=1).astype(F32)
    sub_g = diff_subln_g[:, None, :]

    for li in range(depth):
        need_ctx = li < depth - 1
        last = li == depth - 1
        lam_init = 0.8 - 0.6 * math.exp(-0.3 * li)
        p = _inproj(xl, mods, li, 0, g1, w_in, tabs_lat, total, 0)
        p = _inproj(xc, mods, li, 1, g1, w_in, tabs_ctx, total, n_lat, p_prev=p)
        o_f, o_b = _retention(p, dec, li, n_lat, n_ctx, dq)
        y_c = _attention(p, lam_vecs, sub_g, li, 0, n_lat, 0, total, d, lam_init)
        xl1 = _outproj(p, o_f, o_b, sgu_w_b, sgu_b_col, y_c, 0, w_out_b, xl, mods, li, 0)
        xl_new = _ffn(xl1, mods, li, 0, g2, w_gate, w_up, w_down, fg, last)
        if need_ctx:
            yc_c = _attention(p, lam_vecs, sub_g, li, n_lat, n_ctx, n_lat, n_ctx, d, lam_init)
            xc1 = _outproj(p, o_f, o_b, sgu_w_b, sgu_b_col, yc_c, n_lat, w_out_b, xc, mods, li, 1)
            xc = _ffn(xc1, mods, li, 1, g2, w_gate, w_up, w_down, fg, False)
        xl = xl_new
    return xl[None]
```

```python
import functools
import math

import jax
import jax.numpy as jnp
import numpy as np
from jax import lax
from jax.experimental import pallas as pl
from jax.experimental.pallas import tpu as pltpu

GRID_W = 64
ROPE_BASE = 10000.0
EPS = 1e-6
N_MOD = 6
HEAD = 128
SGU_CHUNK = 128
RET_BLOCK = 256
IN_TILES = 6
ROW_CHUNK = 256
V7X_VMEM_BYTES = 64 * 1024 * 1024
VMEM_RESERVE_BYTES = 2 * 1024 * 1024

F32 = jnp.float32
BF16 = jnp.bfloat16


def _nbytes(shape, dtype):
    return math.prod(shape) * jnp.dtype(dtype).itemsize


def _cparams(sem, windows, scratch=()):
    est = 2 * sum(_nbytes(*w) for w in windows) + sum(_nbytes(*b) for b in scratch)
    limit = min(V7X_VMEM_BYTES - VMEM_RESERVE_BYTES, 2 * est + 16 * 1024 * 1024)
    return pltpu.CompilerParams(dimension_semantics=sem, vmem_limit_bytes=limit)


def _tile(n, want):
    if n <= want:
        return n
    t = want
    while n % t:
        t -= 8
    return t


def _ada_kernel(a_ref, w_ref, b_ref, o_ref, ab_ref, *, rows):
    d, tn = w_ref.shape[1], w_ref.shape[2]

    @pl.when((pl.program_id(0) == 0) & (pl.program_id(1) == 0))
    def _():
        a = a_ref[...]
        a = a * jax.nn.sigmoid(a)
        ab_ref[0] = jnp.broadcast_to(a[:, 0:1], (d, HEAD))
        ab_ref[1] = jnp.broadcast_to(a[:, 1:2], (d, HEAD))

    def body(r, acc):
        r0 = pl.multiple_of(r * rows, rows)
        a0 = ab_ref[0, pl.ds(r0, rows), :]
        a1 = ab_ref[1, pl.ds(r0, rows), :]
        new = []
        for c in range(tn // HEAD):
            w = w_ref[0, pl.ds(r0, rows), c * HEAD:(c + 1) * HEAD]
            new.append(acc[2 * c] + (w * a0).reshape(rows // 8, 8, HEAD).sum(axis=0))
            new.append(acc[2 * c + 1] + (w * a1).reshape(rows // 8, 8, HEAD).sum(axis=0))
        return tuple(new)

    z = jnp.zeros((8, HEAD), F32)
    acc = lax.fori_loop(0, d // rows, body, (z,) * (2 * (tn // HEAD)))
    for c in range(tn // HEAD):
        out = jnp.concatenate([acc[2 * c].sum(axis=0, keepdims=True), acc[2 * c + 1].sum(axis=0, keepdims=True)],
                              axis=0)
        o_ref[0, :, c * HEAD:(c + 1) * HEAD] = out + b_ref[0, :, c * HEAD:(c + 1) * HEAD]


def _ada(c2t, w_ada, b_ada):
    depth, d, n = w_ada.shape
    tn = _tile(n, 1024)
    return pl.pallas_call(
        functools.partial(_ada_kernel, rows=128),
        out_shape=jax.ShapeDtypeStruct((depth, 2, n), F32),
        grid=(depth, n // tn),
        in_specs=[
            pl.BlockSpec((d, 2), lambda l, j: (0, 0)),
            pl.BlockSpec((1, d, tn), lambda l, j: (l, 0, j)),
            pl.BlockSpec((1, 1, tn), lambda l, j: (l, 0, j)),
        ],
        out_specs=pl.BlockSpec((1, 2, tn), lambda l, j: (l, 0, j)),
        scratch_shapes=[pltpu.VMEM((2, d, HEAD), F32)],
        compiler_params=_cparams(("arbitrary", "arbitrary"), [((d, tn), F32), ((d, HEAD), F32)],
                                 [((2, d, HEAD), F32)]),
        name="ada",
    )(c2t, w_ada, b_ada.reshape(depth, 1, n))


def _rope(x, c, s1, s2, shift):
    return x * c + pltpu.roll(x, HEAD - shift, 1) * s1 + pltpu.roll(x, shift, 1) * s2


def _gelu_tanh(x):
    c = 2.0 * math.sqrt(2.0 / math.pi)
    return x * jax.nn.sigmoid(x * (c + (c * 0.044715) * (x * x)))


def _group_rms(x):
    return x * lax.rsqrt(jnp.mean(x * x, axis=-1, keepdims=True) + EPS)


def _modulated_norm(x, g, sh, sc):
    y = x * lax.rsqrt(jnp.mean(x * x, axis=-1, keepdims=True) + EPS) * g
    return y * (1.0 + sc) + sh


def _inproj_kernel(x_ref, sh_ref, sc_ref, g_ref, w_ref, tab_ref, *rest, aliased):
    o_ref, h_ref = rest[1:] if aliased else rest
    j = pl.program_id(1)
    tn = w_ref.shape[1]

    @pl.when(j == 0)
    def _():
        h_ref[...] = _modulated_norm(x_ref[...], g_ref[...], sh_ref[...], sc_ref[...]).astype(BF16)

    def project(*fns):
        rc = min(h_ref.shape[0], ROW_CHUNK)
        part = tn // len(fns)
        slab = min(2 * HEAD, part)
        for c0 in range(0, tn, slab):
            fn = fns[c0 // part]
            w = w_ref[:, c0:c0 + slab].astype(BF16)
            for r in range(h_ref.shape[0] // rc):
                rs = slice(r * rc, (r + 1) * rc)
                acc = jnp.dot(h_ref[rs, :], w, preferred_element_type=F32)
                for hh in range(slab // HEAD):
                    o_ref[rs, c0 + hh * HEAD:c0 + (hh + 1) * HEAD] = fn(
                        acc[:, hh * HEAD:(hh + 1) * HEAD], rs).astype(BF16)

    def rope_ret(scale):
        return lambda a, rs: _rope(a, tab_ref[0, rs, :], tab_ref[1, rs, :], tab_ref[2, rs, :], HEAD // 4) * scale

    def rope_diff(scale):
        return lambda a, rs: _rope(a, tab_ref[3, rs, :], tab_ref[4, rs, :], tab_ref[5, rs, :], HEAD // 8) * scale

    @pl.when(j == 0)
    def _():
        project(lambda a, rs: _gelu_tanh(a), lambda a, rs: _group_rms(_gelu_tanh(a)))

    @pl.when(j == 1)
    def _():
        project(rope_ret(1.0), rope_ret(HEAD ** -0.5))

    @pl.when(j == 2)
    def _():
        project(lambda a, rs: a, lambda a, rs: a * jax.nn.sigmoid(a))

    @pl.when(j == 3)
    def _():
        project(rope_diff((HEAD // 2) ** -0.5 * math.log2(math.e)))

    @pl.when(j == 4)
    def _():
        project(rope_diff(1.0))

    @pl.when(j == 5)
    def _():
        project(lambda a, rs: a)


def _inproj(x, mods, li, row, g, w, tabs, total_rows, row_off, p_prev=None):
    m, d = x.shape
    n = w.shape[2]
    tn = n // IN_TILES
    tm = _tile(m, 1024)
    assert row_off % tm == 0
    ob = row_off // tm
    aliased = p_prev is not None
    in_specs = [
        pl.BlockSpec((tm, d), lambda i, j: (i, 0)),
        pl.BlockSpec((None, None, None, 1, d), lambda i, j: (li, row, 0, 0, 0)),
        pl.BlockSpec((None, None, None, 1, d), lambda i, j: (li, row, 1, 0, 0)),
        pl.BlockSpec((None, 1, d), lambda i, j: (li, 0, 0)),
        pl.BlockSpec((None, d, tn), lambda i, j: (li, 0, j)),
        pl.BlockSpec((6, tm, HEAD), lambda i, j: (0, i, 0)),
    ]
    args = [x, mods, mods, g, w, tabs]
    if aliased:
        in_specs.append(pl.BlockSpec(memory_space=pl.ANY))
        args.append(p_prev)
    return pl.pallas_call(
        functools.partial(_inproj_kernel, aliased=aliased),
        out_shape=jax.ShapeDtypeStruct((total_rows, n), BF16),
        grid=(m // tm, IN_TILES),
        in_specs=in_specs,
        out_specs=pl.BlockSpec((tm, tn), lambda i, j: (ob + i, j)),
        scratch_shapes=[pltpu.VMEM((tm, d), BF16)],
        input_output_aliases={6: 0} if aliased else {},
        compiler_params=_cparams(("parallel", "arbitrary"),
                                 [((tm, d), x.dtype), ((d, tn), w.dtype), ((6, tm, HEAD), F32), ((tm, tn), BF16)],
                                 [((tm, d), BF16)]),
        name="inproj",
    )(*args)


def _log_sigmoid(z):
    return jnp.minimum(z, 0.0) - jnp.log1p(jnp.exp(-jnp.abs(z)))


def _ret_kernel(dec_ref, qf_ref, kf_ref, vf_ref, qb_ref, kb_ref, vb_ref, of_ref, ob_ref,
                state_ref, mask_ref, qd_ref, kd_ref, cd_ref, *, li):
    s = pl.program_id(0)
    rc = qf_ref.shape[0]
    nh = qf_ref.shape[1] // HEAD

    @pl.when(s == 0)
    def _():
        state_ref[...] = jnp.zeros_like(state_ref)
        ii = lax.broadcasted_iota(jnp.int32, (rc, rc), 0).astype(F32)
        jj = lax.broadcasted_iota(jnp.int32, (rc, rc), 1).astype(F32)
        col = lax.broadcasted_iota(jnp.int32, (rc, HEAD), 0).astype(F32)
        for dr in range(2):
            rel = (ii - jj) if dr == 0 else (jj - ii)
            for h in range(nh):
                logit = dec_ref[dr, li, h]
                lg = _log_sigmoid(jnp.full((rc, rc), logit, F32))
                mask_ref[dr, h] = jnp.where(rel >= 0, jnp.exp(jnp.maximum(rel, 0.0) * lg), 0.0)
                lgc = _log_sigmoid(jnp.full((rc, HEAD), logit, F32))
                q_pow = (col + 1.0) if dr == 0 else (rc - col)
                k_pow = (rc - 1.0 - col) if dr == 0 else col
                qd_ref[dr, h] = jnp.exp(q_pow * lgc)
                kd_ref[dr, h] = jnp.exp(k_pow * lgc)
                cd_ref[dr, h] = jnp.exp(rc * _log_sigmoid(jnp.full((HEAD, HEAD), logit, F32)))

    for dr, (q_ref, k_ref, v_ref, o_ref) in enumerate(
            ((qf_ref, kf_ref, vf_ref, of_ref), (qb_ref, kb_ref, vb_ref, ob_ref))):
        for h in range(nh):
            sl = slice(h * HEAD, (h + 1) * HEAD)
            q = q_ref[:, sl]
            k = k_ref[:, sl]
            v = v_ref[:, sl]
            state = state_ref[dr, h]
            scores = lax.dot_general(q, k, (((1,), (1,)), ((), ())), preferred_element_type=F32)
            scores = (scores * mask_ref[dr, h]).astype(BF16)
            qs = (q.astype(F32) * qd_ref[dr, h]).astype(BF16)
            out = jnp.dot(scores, v, preferred_element_type=F32)
            out = out + jnp.dot(qs, state.astype(BF16), preferred_element_type=F32)
            o_ref[:, sl] = out
            ks = (k.astype(F32) * kd_ref[dr, h]).astype(BF16)
            upd = lax.dot_general(ks, v, (((0,), (0,)), ((), ())), preferred_element_type=F32)
            state_ref[dr, h] = state * cd_ref[dr, h] + upd


def _retention(p, dec, li, n_lat, n_ctx, d_ret):
    t = p.shape[0]
    rc = RET_BLOCK
    assert n_lat % rc == 0 and n_ctx % rc == 0
    nb, nlb, ncb = t // rc, n_lat // rc, n_ctx // rc
    nh = d_ret // HEAD
    cq, ck, cv = 2, 3, 4

    def fmap(col):
        return lambda s: (jnp.where(s < ncb, nlb + s, s - ncb), col)

    def bmap(col):
        return lambda s: (nb - 1 - s, col)

    blk = lambda im: pl.BlockSpec((rc, d_ret), im)
    return pl.pallas_call(
        functools.partial(_ret_kernel, li=li),
        out_shape=[jax.ShapeDtypeStruct((t, d_ret), F32)] * 2,
        grid=(nb,),
        in_specs=[pl.BlockSpec(memory_space=pltpu.SMEM),
                  blk(fmap(cq)), blk(fmap(ck)), blk(fmap(cv)),
                  blk(bmap(cq)), blk(bmap(ck)), blk(bmap(cv))],
        out_specs=[blk(fmap(0)), blk(bmap(0))],
        scratch_shapes=[
            pltpu.VMEM((2, nh, HEAD, HEAD), F32),
            pltpu.VMEM((2, nh, rc, rc), F32),
            pltpu.VMEM((2, nh, rc, HEAD), F32),
            pltpu.VMEM((2, nh, rc, HEAD), F32),
            pltpu.VMEM((2, nh, HEAD, HEAD), F32),
        ],
        compiler_params=_cparams(("arbitrary",), [((rc, d_ret), BF16)] * 6 + [((rc, d_ret), F32)] * 2,
                                 [((2, nh, rc, rc), F32), ((2, nh, rc, HEAD), F32), ((2, nh, rc, HEAD), F32)]),
        name="retention",
    )(dec, p, p, p, p, p, p)


N_BUF = 4
SM_LAG = 1
PV_LAG = 3
KEY_CHUNK = 2816
FIRST_KEY_CHUNK = 256


def _key_chunks(n_k):
    if n_k <= KEY_CHUNK:
        return [(0, n_k)]
    chunks, r0 = [(0, FIRST_KEY_CHUNK)], FIRST_KEY_CHUNK
    while r0 < n_k:
        size = min(KEY_CHUNK, n_k - r0)
        chunks.append((r0, size))
        r0 += size
    return chunks


def _attn_kernel(q_ref, k_ref, v_ref, lam_ref, g_ref, o_ref, *scratch, lam_init):
    tq = q_ref.shape[0]
    chunks = _key_chunks(k_ref.shape[0])
    nkv = len(chunks)
    s_bufs, p_bufs, a_bufs, pm_bufs = (scratch[i * N_BUF:(i + 1) * N_BUF] for i in range(4))
    acc_ref, m_ref, qs_ref = scratch[4 * N_BUF:]
    q = q_ref[...]
    lane = lax.broadcasted_iota(jnp.int32, q.shape, 1)
    zero = jnp.zeros_like(q)
    qs_ref[:tq] = jnp.where(lane < HEAD // 2, q, zero)
    qs_ref[tq:] = jnp.where(lane >= HEAD // 2, q, zero)
    acc_ref[...] = jnp.zeros_like(acc_ref)
    m_ref[...] = jnp.full(m_ref.shape, -jnp.inf, F32)

    def scores(c):
        r0, size = chunks[c]
        s = lax.dot_general(qs_ref[...], k_ref[r0:r0 + size, :], (((1,), (1,)), ((), ())),
                            preferred_element_type=F32)
        s_bufs[c % N_BUF][:, :size] = s
        pm = s[:, :HEAD]
        for j in range(1, size // HEAD):
            pm = jnp.maximum(pm, s[:, j * HEAD:(j + 1) * HEAD])
        pm_bufs[c % N_BUF][...] = pm

    def softmax(c):
        size = chunks[c][1]
        s_ref, p_ref = s_bufs[c % N_BUF], p_bufs[c % N_BUF]
        m_old = m_ref[...]
        m_new = jnp.maximum(m_old, jnp.broadcast_to(pm_bufs[c % N_BUF][...].max(axis=1, keepdims=True),
                                                    m_old.shape))
        a_bufs[c % N_BUF][...] = jnp.exp2(m_old - m_new)
        m_ref[...] = m_new
        for j in range(size // HEAD):
            sl = slice(j * HEAD, (j + 1) * HEAD)
            p_ref[:, sl] = jnp.exp2(s_ref[:, sl] - m_new).astype(BF16)

    def weighted_values(c):
        r0, size = chunks[c]
        v_ext = jnp.concatenate([v_ref[r0:r0 + size, :], jnp.ones((size, HEAD), BF16)], axis=1)
        pv = jnp.dot(p_bufs[c % N_BUF][:, :size], v_ext, preferred_element_type=F32)
        alpha = a_bufs[c % N_BUF][...]
        acc_ref[:, :HEAD] = alpha * acc_ref[:, :HEAD] + pv[:, :HEAD]
        acc_ref[:, HEAD:] = alpha * acc_ref[:, HEAD:] + pv[:, HEAD:]

    for t in range(nkv + PV_LAG):
        if t < nkv:
            scores(t)
        if 0 <= t - SM_LAG < nkv:
            softmax(t - SM_LAG)
        if 0 <= t - PV_LAG < nkv:
            weighted_values(t - PV_LAG)

    lv = lam_ref[...]
    lam = (jnp.exp(jnp.sum(lv[0:1] * lv[1:2], axis=1, keepdims=True))
           - jnp.exp(jnp.sum(lv[2:3] * lv[3:4], axis=1, keepdims=True)) + lam_init)
    on = acc_ref[:, :HEAD] / acc_ref[:, HEAD:]
    o = on[:tq] - lam * on[tq:]
    y = o * lax.rsqrt(jnp.mean(o * o, axis=-1, keepdims=True) + EPS) * g_ref[...]
    o_ref[...] = (y * (1.0 - lam_init)).astype(BF16)


def _attention(p, lam_vecs, subln_g, li, q_row0, n_q, k_row0, n_k, d_model, lam_init):
    d_diff = d_model // 2
    nh = d_diff // HEAD
    quarter = (d_model // 4) // HEAD
    cq, ck, cv = 6 * quarter, 8 * quarter, 10 * quarter
    tq = _tile(n_q, 512)
    tk = max(size for _, size in _key_chunks(n_k))
    assert n_k % HEAD == 0
    assert q_row0 % tq == 0 and k_row0 % n_k == 0
    qb, kb = q_row0 // tq, k_row0 // n_k
    return pl.pallas_call(
        functools.partial(_attn_kernel, lam_init=lam_init),
        out_shape=jax.ShapeDtypeStruct((n_q, d_diff), BF16),
        grid=(nh, n_q // tq),
        in_specs=[
            pl.BlockSpec((tq, HEAD), lambda h, i: (qb + i, cq + h)),
            pl.BlockSpec((n_k, HEAD), lambda h, i: (kb, ck + h)),
            pl.BlockSpec((n_k, HEAD), lambda h, i: (kb, cv + h)),
            pl.BlockSpec((None,) + lam_vecs.shape[1:], lambda h, i: (li, 0, 0)),
            pl.BlockSpec((None, 1, HEAD), lambda h, i: (li, 0, 0)),
        ],
        out_specs=pl.BlockSpec((tq, HEAD), lambda h, i: (i, h)),
        scratch_shapes=([pltpu.VMEM((2 * tq, tk), F32)] * N_BUF + [pltpu.VMEM((2 * tq, tk), BF16)] * N_BUF
                        + [pltpu.VMEM((2 * tq, HEAD), F32)] * (2 * N_BUF)
                        + [pltpu.VMEM((2 * tq, 2 * HEAD), F32), pltpu.VMEM((2 * tq, HEAD), F32),
                           pltpu.VMEM((2 * tq, HEAD), BF16)]),
        compiler_params=_cparams(("parallel", "parallel"), [((n_k, HEAD), BF16)] * 2 + [((tq, HEAD), BF16)] * 2,
                                 [((2 * tq, tk), F32)] * N_BUF + [((2 * tq, tk), BF16)] * N_BUF
                                 + [((2 * tq, HEAD), F32)] * (2 * N_BUF + 4)),
        name="diffattn",
    )(p, p, p, lam_vecs, subln_g)


def _outproj_kernel(u_ref, v_ref, gate_ref, of_ref, ob_ref, ws_ref, bs_ref, yc_ref, wa_ref, wb_ref, wc_ref,
                    x_ref, g1_ref, o_ref, ya_ref, yb_ref):
    tm = x_ref.shape[0]
    for n in range(tm // SGU_CHUNK):
        rs = slice(n * SGU_CHUNK, (n + 1) * SGU_CHUNK)
        for g in range(u_ref.shape[1] // HEAD):
            cs = slice(g * HEAD, (g + 1) * HEAD)
            mixed = jnp.dot(ws_ref[g], v_ref[rs, cs], preferred_element_type=F32) + bs_ref[g]
            ya_ref[rs, cs] = (u_ref[rs, cs].astype(F32) * mixed).astype(BF16)
    for h in range(of_ref.shape[1] // HEAD):
        cs = slice(h * HEAD, (h + 1) * HEAD)
        o = of_ref[:, cs] + ob_ref[:, cs]
        yb_ref[:, cs] = (_group_rms(o) * gate_ref[:, cs].astype(F32)).astype(BF16)
    acc = jnp.dot(ya_ref[...], wa_ref[...], preferred_element_type=F32)
    acc = acc + jnp.dot(yb_ref[...], wb_ref[...], preferred_element_type=F32)
    acc = acc + jnp.dot(yc_ref[...], wc_ref[...], preferred_element_type=F32)
    o_ref[...] = x_ref[...] + g1_ref[...] * acc


def _outproj(p, o_f, o_b, ws, bs, yc, y_row0, w, x, mods, li, row):
    m, d = x.shape
    dq = d // 4
    tm = _tile(m, 512)
    assert y_row0 % tm == 0 and tm % SGU_CHUNK == 0
    yb0 = y_row0 // tm
    tok = lambda col: pl.BlockSpec((tm, dq), lambda i: (yb0 + i, col))
    layer = lambda a: pl.BlockSpec((None,) + a.shape[1:], lambda i: (li,) + (0,) * (a.ndim - 1))
    return pl.pallas_call(
        _outproj_kernel,
        out_shape=jax.ShapeDtypeStruct((m, d), F32),
        grid=(m // tm,),
        in_specs=[
            tok(0), tok(1), tok(5), tok(0), tok(0), layer(ws), layer(bs),
            pl.BlockSpec((tm, 2 * dq), lambda i: (i, 0)),
            pl.BlockSpec((None, dq, d), lambda i: (li, 0, 0)),
            pl.BlockSpec((None, dq, d), lambda i: (li, 1, 0)),
            pl.BlockSpec((None, 2 * dq, d), lambda i: (li, 1, 0)),
            pl.BlockSpec((tm, d), lambda i: (i, 0)),
            pl.BlockSpec((None, None, None, 1, d), lambda i: (li, row, 2, 0, 0)),
        ],
        out_specs=pl.BlockSpec((tm, d), lambda i: (i, 0)),
        scratch_shapes=[pltpu.VMEM((tm, dq), BF16), pltpu.VMEM((tm, dq), BF16)],
        compiler_params=_cparams(("parallel",), [((tm, d), BF16), ((d, d), BF16), ((tm, d), F32), ((tm, d), F32),
                                                 ((tm, dq), F32), ((tm, dq), F32)], [((tm, d), BF16)]),
        name="outproj",
    )(p, p, p, o_f, o_b, ws, bs, yc, w, w, w, x, mods)


def _ffn_kernel(x_ref, sh_ref, sc_ref, gate_ref, g_ref, wg_ref, wu_ref, wd_ref, fg_ref, o_ref, h_ref,
                *, final_norm, n_ff_steps):
    f = pl.program_id(1)

    nf = pl.num_programs(1)

    def swiglu_slice(first, last):
        h = h_ref[...]
        a = jnp.dot(h, wg_ref[...].astype(BF16), preferred_element_type=F32)
        b = jnp.dot(h, wu_ref[...].astype(BF16), preferred_element_type=F32)
        act = (a * jax.nn.sigmoid(a) * b).astype(BF16)
        acc = jnp.dot(act, wd_ref[...].astype(BF16), preferred_element_type=F32)
        if not first:
            acc = o_ref[...] + acc
        if last:
            acc = x_ref[...] + gate_ref[...] * acc
            if final_norm:
                acc = acc * lax.rsqrt(jnp.mean(acc * acc, axis=-1, keepdims=True) + EPS) * fg_ref[...]
        o_ref[...] = acc

    @pl.when(f == 0)
    def _():
        h_ref[...] = _modulated_norm(x_ref[...], g_ref[...], sh_ref[...], sc_ref[...]).astype(BF16)
        swiglu_slice(True, n_ff_steps == 1)

    if n_ff_steps > 2:
        @pl.when((f != 0) & (f != nf - 1))
        def _():
            swiglu_slice(False, False)

    if n_ff_steps > 1:
        @pl.when(f == nf - 1)
        def _():
            swiglu_slice(False, True)


def _ffn(x, mods, li, row, g, wg, wu, wd, final_g, final_norm):
    m, d = x.shape
    ff = wg.shape[2]
    tm = _tile(m, 1024)
    tf = _tile(ff, 256)
    mod = lambda c: pl.BlockSpec((None, None, None, 1, d), lambda i, f: (li, row, c, 0, 0))
    return pl.pallas_call(
        functools.partial(_ffn_kernel, final_norm=final_norm, n_ff_steps=ff // tf),
        out_shape=jax.ShapeDtypeStruct((m, d), F32),
        grid=(m // tm, ff // tf),
        in_specs=[
            pl.BlockSpec((tm, d), lambda i, f: (i, 0), pipeline_mode=pl.Buffered(1)),
            mod(3), mod(4), mod(5),
            pl.BlockSpec((None, 1, d), lambda i, f: (li, 0, 0)),
            pl.BlockSpec((None, d, tf), lambda i, f: (li, 0, f)),
            pl.BlockSpec((None, d, tf), lambda i, f: (li, 0, f)),
            pl.BlockSpec((None, tf, d), lambda i, f: (li, f, 0)),
            pl.BlockSpec((1, d), lambda i, f: (0, 0)),
        ],
        out_specs=pl.BlockSpec((tm, d), lambda i, f: (i, 0)),
        scratch_shapes=[pltpu.VMEM((tm, d), BF16)],
        compiler_params=_cparams(("parallel", "arbitrary"),
                                 [((tm, d), F32), ((tm, d), F32)] + [((d, tf), wg.dtype)] * 3, [((tm, d), BF16)]),
        name="ffn",
    )(x, mods, mods, mods, g, wg, wu, wd, final_g)


def _rope_tables(n_rows, identity):
    def one(width):
        half = width // 2
        lane = np.arange(HEAD)
        sub = lane // width
        idx = lane % width
        first = idx < half
        freqs = np.float32(ROPE_BASE) ** (-(idx % half).astype(np.float32) / np.float32(half))
        t = np.arange(n_rows)
        pos = np.where((sub % 2) == 0, (t // GRID_W)[:, None], (t % GRID_W)[:, None]).astype(np.float32)
        ang = pos * freqs[None, :]
        cos, sin = np.cos(ang), np.sin(ang)
        if identity:
            cos, sin = np.ones_like(cos), np.zeros_like(sin)
        return [cos, np.where(first[None, :], -sin, 0.0), np.where(first[None, :], 0.0, sin)]
    return jnp.asarray(np.stack(one(HEAD // 2) + one(HEAD // 4)).astype(np.float32))


def kernel(x, c, ctx, c_ctx, w_ada, b_ada, norm1_g, w_in, sgu_w, sgu_b, ret_decay_fwd, ret_decay_bwd,
           diff_lambda_q1, diff_lambda_k1, diff_lambda_q2, diff_lambda_k2, diff_subln_g, w_out, norm2_g,
           w_gate, w_up, w_down, final_g):
    assert x.shape[0] == 1 and ctx.shape[0] == 1
    depth = w_ada.shape[0]
    xl, xc = x[0], ctx[0]
    n_lat, d = xl.shape
    n_ctx = xc.shape[0]
    total = n_lat + n_ctx
    dq = d // 4

    mods = _ada(jnp.concatenate([c, c_ctx[None, :]], axis=0).T, w_ada, b_ada)
    mods = mods.reshape(depth, 2, N_MOD, 1, d)
    tabs_lat = _rope_tables(n_lat, identity=False)
    tabs_ctx = _rope_tables(n_ctx, identity=True)
    w_out_b = w_out.astype(BF16)
    sgu_w_b = sgu_w.astype(BF16)
    sgu_b_col = sgu_b[..., None]
    g1, g2, fg = norm1_g[:, None, :], norm2_g[:, None, :], final_g[None, :]
    dec = jnp.stack([ret_decay_fwd, ret_decay_bwd]).astype(F32)
    lam_vecs = jnp.stack([diff_lambda_q1, diff_lambda_k1, diff_lambda_q2, diff_lambda_k2], axis=1).astype(F32)
    sub_g = diff_subln_g[:, None, :]

    for li in range(depth):
        need_ctx = li < depth - 1
        last = li == depth - 1
        lam_init = 0.8 - 0.6 * math.exp(-0.3 * li)
        p = _inproj(xl, mods, li, 0, g1, w_in, tabs_lat, total, 0)
        p = _inproj(xc, mods, li, 1, g1, w_in, tabs_ctx, total, n_lat, p_prev=p)
        o_f, o_b = _retention(p, dec, li, n_lat, n_ctx, dq)
        y_c = _attention(p, lam_vecs, sub_g, li, 0, n_lat, 0, total, d, lam_init)
        xl1 = _outproj(p, o_f, o_b, sgu_w_b, sgu_b_col, y_c, 0, w_out_b, xl, mods, li, 0)
        xl_new = _ffn(xl1, mods, li, 0, g2, w_gate, w_up, w_down, fg, last)
        if need_ctx:
            yc_c = _attention(p, lam_vecs, sub_g, li, n_lat, n_ctx, n_lat, n_ctx, d, lam_init)
            xc1 = _outproj(p, o_f, o_b, sgu_w_b, sgu_b_col, yc_c, n_lat, w_out_b, xc, mods, li, 1)
            xc = _ffn(xc1, mods, li, 1, g2, w_gate, w_up, w_down, fg, False)
        xl = xl_new
    return xl[None]
```

```python
import functools
import math

import jax
import jax.numpy as jnp
import numpy as np
from jax import lax
from jax.experimental import pallas as pl
from jax.experimental.pallas import tpu as pltpu

GRID_W = 64
ROPE_BASE = 10000.0
EPS = 1e-6
N_MOD = 6
HEAD = 128
SGU_CHUNK = 128
RET_BLOCK = 256
IN_TILES = 6
ROW_CHUNK = 256
V7X_VMEM_BYTES = 64 * 1024 * 1024
VMEM_RESERVE_BYTES = 2 * 1024 * 1024

F32 = jnp.float32
BF16 = jnp.bfloat16


def _nbytes(shape, dtype):
    return math.prod(shape) * jnp.dtype(dtype).itemsize


def _cparams(sem, windows, scratch=()):
    est = 2 * sum(_nbytes(*w) for w in windows) + sum(_nbytes(*b) for b in scratch)
    limit = min(V7X_VMEM_BYTES - VMEM_RESERVE_BYTES, 2 * est + 16 * 1024 * 1024)
    return pltpu.CompilerParams(dimension_semantics=sem, vmem_limit_bytes=limit)


def _tile(n, want):
    if n <= want:
        return n
    t = want
    while n % t:
        t -= 8
    return t


def _ada_kernel(a_ref, w_ref, b_ref, o_ref, ab_ref, *, rows):
    d, tn = w_ref.shape[1], w_ref.shape[2]

    @pl.when((pl.program_id(0) == 0) & (pl.program_id(1) == 0))
    def _():
        a = a_ref[...]
        a = a * jax.nn.sigmoid(a)
        ab_ref[0] = jnp.broadcast_to(a[:, 0:1], (d, HEAD))
        ab_ref[1] = jnp.broadcast_to(a[:, 1:2], (d, HEAD))

    def body(r, acc):
        r0 = pl.multiple_of(r * rows, rows)
        a0 = ab_ref[0, pl.ds(r0, rows), :]
        a1 = ab_ref[1, pl.ds(r0, rows), :]
        new = []
        for c in range(tn // HEAD):
            w = w_ref[0, pl.ds(r0, rows), c * HEAD:(c + 1) * HEAD]
            new.append(acc[2 * c] + (w * a0).reshape(rows // 8, 8, HEAD).sum(axis=0))
            new.append(acc[2 * c + 1] + (w * a1).reshape(rows // 8, 8, HEAD).sum(axis=0))
        return tuple(new)

    z = jnp.zeros((8, HEAD), F32)
    acc = lax.fori_loop(0, d // rows, body, (z,) * (2 * (tn // HEAD)))
    for c in range(tn // HEAD):
        out = jnp.concatenate([acc[2 * c].sum(axis=0, keepdims=True), acc[2 * c + 1].sum(axis=0, keepdims=True)],
                              axis=0)
        o_ref[0, :, c * HEAD:(c + 1) * HEAD] = out + b_ref[0, :, c * HEAD:(c + 1) * HEAD]


def _ada(c2t, w_ada, b_ada):
    depth, d, n = w_ada.shape
    tn = _tile(n, 1024)
    return pl.pallas_call(
        functools.partial(_ada_kernel, rows=128),
        out_shape=jax.ShapeDtypeStruct((depth, 2, n), F32),
        grid=(depth, n // tn),
        in_specs=[
            pl.BlockSpec((d, 2), lambda l, j: (0, 0)),
            pl.BlockSpec((1, d, tn), lambda l, j: (l, 0, j)),
            pl.BlockSpec((1, 1, tn), lambda l, j: (l, 0, j)),
        ],
        out_specs=pl.BlockSpec((1, 2, tn), lambda l, j: (l, 0, j)),
        scratch_shapes=[pltpu.VMEM((2, d, HEAD), F32)],
        compiler_params=_cparams(("arbitrary", "arbitrary"), [((d, tn), F32), ((d, HEAD), F32)],
                                 [((2, d, HEAD), F32)]),
        name="ada",
    )(c2t, w_ada, b_ada.reshape(depth, 1, n))


def _rope(x, c, s1, s2, shift):
    return x * c + pltpu.roll(x, HEAD - shift, 1) * s1 + pltpu.roll(x, shift, 1) * s2


def _gelu_tanh(x):
    c = 2.0 * math.sqrt(2.0 / math.pi)
    return x * jax.nn.sigmoid(x * (c + (c * 0.044715) * (x * x)))


def _group_rms(x):
    return x * lax.rsqrt(jnp.mean(x * x, axis=-1, keepdims=True) + EPS)


def _modulated_norm(x, g, sh, sc):
    y = x * lax.rsqrt(jnp.mean(x * x, axis=-1, keepdims=True) + EPS) * g
    return y * (1.0 + sc) + sh


def _inproj_kernel(x_ref, sh_ref, sc_ref, g_ref, w_ref, tab_ref, *rest, aliased):
    o_ref, h_ref = rest[1:] if aliased else rest
    j = pl.program_id(1)
    tn = w_ref.shape[1]

    @pl.when(j == 0)
    def _():
        h_ref[...] = _modulated_norm(x_ref[...], g_ref[...], sh_ref[...], sc_ref[...]).astype(BF16)

    def project(*fns):
        rc = min(h_ref.shape[0], ROW_CHUNK)
        part = tn // len(fns)
        slab = min(2 * HEAD, part)
        for c0 in range(0, tn, slab):
            fn = fns[c0 // part]
            w = w_ref[:, c0:c0 + slab].astype(BF16)
            for r in range(h_ref.shape[0] // rc):
                rs = slice(r * rc, (r + 1) * rc)
                acc = jnp.dot(h_ref[rs, :], w, preferred_element_type=F32)
                for hh in range(slab // HEAD):
                    o_ref[rs, c0 + hh * HEAD:c0 + (hh + 1) * HEAD] = fn(
                        acc[:, hh * HEAD:(hh + 1) * HEAD], rs).astype(BF16)

    def rope_ret(scale):
        return lambda a, rs: _rope(a, tab_ref[0, rs, :], tab_ref[1, rs, :], tab_ref[2, rs, :], HEAD // 4) * scale

    def rope_diff(scale):
        return lambda a, rs: _rope(a, tab_ref[3, rs, :], tab_ref[4, rs, :], tab_ref[5, rs, :], HEAD // 8) * scale

    @pl.when(j == 0)
    def _():
        project(lambda a, rs: _gelu_tanh(a), lambda a, rs: _group_rms(_gelu_tanh(a)))

    @pl.when(j == 1)
    def _():
        project(rope_ret(1.0), rope_ret(HEAD ** -0.5))

    @pl.when(j == 2)
    def _():
        project(lambda a, rs: a, lambda a, rs: a * jax.nn.sigmoid(a))

    @pl.when(j == 3)
    def _():
        project(rope_diff((HEAD // 2) ** -0.5 * math.log2(math.e)))

    @pl.when(j == 4)
    def _():
        project(rope_diff(1.0))

    @pl.when(j == 5)
    def _():
        project(lambda a, rs: a)


def _inproj(x, mods, li, row, g, w, tabs, total_rows, row_off, p_prev=None):
    m, d = x.shape
    n = w.shape[2]
    tn = n // IN_TILES
    tm = _tile(m, 1024)
    assert row_off % tm == 0
    ob = row_off // tm
    aliased = p_prev is not None
    in_specs = [
        pl.BlockSpec((tm, d), lambda i, j: (i, 0)),
        pl.BlockSpec((None, None, None, 1, d), lambda i, j: (li, row, 0, 0, 0)),
        pl.BlockSpec((None, None, None, 1, d), lambda i, j: (li, row, 1, 0, 0)),
        pl.BlockSpec((None, 1, d), lambda i, j: (li, 0, 0)),
        pl.BlockSpec((None, d, tn), lambda i, j: (li, 0, j)),
        pl.BlockSpec((6, tm, HEAD), lambda i, j: (0, i, 0)),
    ]
    args = [x, mods, mods, g, w, tabs]
    if aliased:
        in_specs.append(pl.BlockSpec(memory_space=pl.ANY))
        args.append(p_prev)
    return pl.pallas_call(
        functools.partial(_inproj_kernel, aliased=aliased),
        out_shape=jax.ShapeDtypeStruct((total_rows, n), BF16),
        grid=(m // tm, IN_TILES),
        in_specs=in_specs,
        out_specs=pl.BlockSpec((tm, tn), lambda i, j: (ob + i, j)),
        scratch_shapes=[pltpu.VMEM((tm, d), BF16)],
        input_output_aliases={6: 0} if aliased else {},
        compiler_params=_cparams(("parallel", "arbitrary"),
                                 [((tm, d), x.dtype), ((d, tn), w.dtype), ((6, tm, HEAD), F32), ((tm, tn), BF16)],
                                 [((tm, d), BF16)]),
        name="inproj",
    )(*args)


def _log_sigmoid(z):
    return jnp.minimum(z, 0.0) - jnp.log1p(jnp.exp(-jnp.abs(z)))


def _ret_kernel(dec_ref, qf_ref, kf_ref, vf_ref, qb_ref, kb_ref, vb_ref, of_ref, ob_ref,
                state_ref, mask_ref, qd_ref, kd_ref, cd_ref, *, li):
    s = pl.program_id(0)
    rc = qf_ref.shape[0]
    nh = qf_ref.shape[1] // HEAD

    @pl.when(s == 0)
    def _():
        state_ref[...] = jnp.zeros_like(state_ref)
        ii = lax.broadcasted_iota(jnp.int32, (rc, rc), 0).astype(F32)
        jj = lax.broadcasted_iota(jnp.int32, (rc, rc), 1).astype(F32)
        col = lax.broadcasted_iota(jnp.int32, (rc, HEAD), 0).astype(F32)
        for dr in range(2):
            rel = (ii - jj) if dr == 0 else (jj - ii)
            for h in range(nh):
                logit = dec_ref[dr, li, h]
                lg = _log_sigmoid(jnp.full((rc, rc), logit, F32))
                mask_ref[dr, h] = jnp.where(rel >= 0, jnp.exp(jnp.maximum(rel, 0.0) * lg), 0.0)
                lgc = _log_sigmoid(jnp.full((rc, HEAD), logit, F32))
                q_pow = (col + 1.0) if dr == 0 else (rc - col)
                k_pow = (rc - 1.0 - col) if dr == 0 else col
                qd_ref[dr, h] = jnp.exp(q_pow * lgc)
                kd_ref[dr, h] = jnp.exp(k_pow * lgc)
                cd_ref[dr, h] = jnp.exp(rc * _log_sigmoid(jnp.full((HEAD, HEAD), logit, F32)))

    for dr, (q_ref, k_ref, v_ref, o_ref) in enumerate(
            ((qf_ref, kf_ref, vf_ref, of_ref), (qb_ref, kb_ref, vb_ref, ob_ref))):
        for h in range(nh):
            sl = slice(h * HEAD, (h + 1) * HEAD)
            q = q_ref[:, sl]
            k = k_ref[:, sl]
            v = v_ref[:, sl]
            state = state_ref[dr, h]
            scores = lax.dot_general(q, k, (((1,), (1,)), ((), ())), preferred_element_type=F32)
            scores = (scores * mask_ref[dr, h]).astype(BF16)
            qs = (q.astype(F32) * qd_ref[dr, h]).astype(BF16)
            out = jnp.dot(scores, v, preferred_element_type=F32)
            out = out + jnp.dot(qs, state.astype(BF16), preferred_element_type=F32)
            o_ref[:, sl] = out
            ks = (k.astype(F32) * kd_ref[dr, h]).astype(BF16)
            upd = lax.dot_general(ks, v, (((0,), (0,)), ((), ())), preferred_element_type=F32)
            state_ref[dr, h] = state * cd_ref[dr, h] + upd


def _retention(p, dec, li, n_lat, n_ctx, d_ret):
    t = p.shape[0]
    rc = RET_BLOCK
    assert n_lat % rc == 0 and n_ctx % rc == 0
    nb, nlb, ncb = t // rc, n_lat // rc, n_ctx // rc
    nh = d_ret // HEAD
    cq, ck, cv = 2, 3, 4

    def fmap(col):
        return lambda s: (jnp.where(s < ncb, nlb + s, s - ncb), col)

    def bmap(col):
        return lambda s: (nb - 1 - s, col)

    blk = lambda im: pl.BlockSpec((rc, d_ret), im)
    return pl.pallas_call(
        functools.partial(_ret_kernel, li=li),
        out_shape=[jax.ShapeDtypeStruct((t, d_ret), F32)] * 2,
        grid=(nb,),
        in_specs=[pl.BlockSpec(memory_space=pltpu.SMEM),
                  blk(fmap(cq)), blk(fmap(ck)), blk(fmap(cv)),
                  blk(bmap(cq)), blk(bmap(ck)), blk(bmap(cv))],
        out_specs=[blk(fmap(0)), blk(bmap(0))],
        scratch_shapes=[
            pltpu.VMEM((2, nh, HEAD, HEAD), F32),
            pltpu.VMEM((2, nh, rc, rc), F32),
            pltpu.VMEM((2, nh, rc, HEAD), F32),
            pltpu.VMEM((2, nh, rc, HEAD), F32),
            pltpu.VMEM((2, nh, HEAD, HEAD), F32),
        ],
        compiler_params=_cparams(("arbitrary",), [((rc, d_ret), BF16)] * 6 + [((rc, d_ret), F32)] * 2,
                                 [((2, nh, rc, rc), F32), ((2, nh, rc, HEAD), F32), ((2, nh, rc, HEAD), F32)]),
        name="retention",
    )(dec, p, p, p, p, p, p)


N_BUF = 4
SM_LAG = 1
PV_LAG = 3
KEY_CHUNK = 3328
FIRST_KEY_CHUNK = 256


def _key_chunks(n_k):
    if n_k <= KEY_CHUNK:
        return [(0, n_k)]
    chunks, r0 = [(0, FIRST_KEY_CHUNK)], FIRST_KEY_CHUNK
    while r0 < n_k:
        size = min(KEY_CHUNK, n_k - r0)
        chunks.append((r0, size))
        r0 += size
    return chunks


def _attn_kernel(q_ref, k_ref, v_ref, lam_ref, g_ref, o_ref, *scratch, lam_init):
    tq = q_ref.shape[0]
    chunks = _key_chunks(k_ref.shape[0])
    nkv = len(chunks)
    s_bufs, p_bufs, a_bufs, pm_bufs = (scratch[i * N_BUF:(i + 1) * N_BUF] for i in range(4))
    acc_ref, m_ref, qs_ref = scratch[4 * N_BUF:]
    q = q_ref[...]
    lane = lax.broadcasted_iota(jnp.int32, q.shape, 1)
    zero = jnp.zeros_like(q)
    qs_ref[:tq] = jnp.where(lane < HEAD // 2, q, zero)
    qs_ref[tq:] = jnp.where(lane >= HEAD // 2, q, zero)
    acc_ref[...] = jnp.zeros_like(acc_ref)
    m_ref[...] = jnp.full(m_ref.shape, -jnp.inf, F32)

    def scores(c):
        r0, size = chunks[c]
        s = lax.dot_general(qs_ref[...], k_ref[r0:r0 + size, :], (((1,), (1,)), ((), ())),
                            preferred_element_type=F32)
        s_bufs[c % N_BUF][:, :size] = s
        pm = s[:, :HEAD]
        for j in range(1, size // HEAD):
            pm = jnp.maximum(pm, s[:, j * HEAD:(j + 1) * HEAD])
        pm_bufs[c % N_BUF][...] = pm

    def softmax(c):
        size = chunks[c][1]
        s_ref, p_ref = s_bufs[c % N_BUF], p_bufs[c % N_BUF]
        m_old = m_ref[...]
        m_new = jnp.maximum(m_old, jnp.broadcast_to(pm_bufs[c % N_BUF][...].max(axis=1, keepdims=True),
                                                    m_old.shape))
        a_bufs[c % N_BUF][...] = jnp.exp2(m_old - m_new)
        m_ref[...] = m_new
        for j in range(size // HEAD):
            sl = slice(j * HEAD, (j + 1) * HEAD)
            p_ref[:, sl] = jnp.exp2(s_ref[:, sl] - m_new).astype(BF16)

    def weighted_values(c):
        r0, size = chunks[c]
        v_ext = jnp.concatenate([v_ref[r0:r0 + size, :], jnp.ones((size, HEAD), BF16)], axis=1)
        pv = jnp.dot(p_bufs[c % N_BUF][:, :size], v_ext, preferred_element_type=F32)
        alpha = a_bufs[c % N_BUF][...]
        acc_ref[:, :HEAD] = alpha * acc_ref[:, :HEAD] + pv[:, :HEAD]
        acc_ref[:, HEAD:] = alpha * acc_ref[:, HEAD:] + pv[:, HEAD:]

    for t in range(nkv + PV_LAG):
        if t < nkv:
            scores(t)
        if 0 <= t - SM_LAG < nkv:
            softmax(t - SM_LAG)
        if 0 <= t - PV_LAG < nkv:
            weighted_values(t - PV_LAG)

    lv = lam_ref[...]
    lam = (jnp.exp(jnp.sum(lv[0:1] * lv[1:2], axis=1, keepdims=True))
           - jnp.exp(jnp.sum(lv[2:3] * lv[3:4], axis=1, keepdims=True)) + lam_init)
    on = acc_ref[:, :HEAD] / acc_ref[:, HEAD:]
    o = on[:tq] - lam * on[tq:]
    y = o * lax.rsqrt(jnp.mean(o * o, axis=-1, keepdims=True) + EPS) * g_ref[...]
    o_ref[...] = (y * (1.0 - lam_init)).astype(BF16)


def _attention(p, lam_vecs, subln_g, li, q_row0, n_q, k_row0, n_k, d_model, lam_init):
    d_diff = d_model // 2
    nh = d_diff // HEAD
    quarter = (d_model // 4) // HEAD
    cq, ck, cv = 6 * quarter, 8 * quarter, 10 * quarter
    tq = _tile(n_q, 512)
    tk = max(size for _, size in _key_chunks(n_k))
    assert n_k % HEAD == 0
    assert q_row0 % tq == 0 and k_row0 % n_k == 0
    qb, kb = q_row0 // tq, k_row0 // n_k
    return pl.pallas_call(
        functools.partial(_attn_kernel, lam_init=lam_init),
        out_shape=jax.ShapeDtypeStruct((n_q, d_diff), BF16),
        grid=(nh, n_q // tq),
        in_specs=[
            pl.BlockSpec((tq, HEAD), lambda h, i: (qb + i, cq + h)),
            pl.BlockSpec((n_k, HEAD), lambda h, i: (kb, ck + h)),
            pl.BlockSpec((n_k, HEAD), lambda h, i: (kb, cv + h)),
            pl.BlockSpec((None,) + lam_vecs.shape[1:], lambda h, i: (li, 0, 0)),
            pl.BlockSpec((None, 1, HEAD), lambda h, i: (li, 0, 0)),
        ],
        out_specs=pl.BlockSpec((tq, HEAD), lambda h, i: (i, h)),
        scratch_shapes=([pltpu.VMEM((2 * tq, tk), F32)] * N_BUF + [pltpu.VMEM((2 * tq, tk), BF16)] * N_BUF
                        + [pltpu.VMEM((2 * tq, HEAD), F32)] * (2 * N_BUF)
                        + [pltpu.VMEM((2 * tq, 2 * HEAD), F32), pltpu.VMEM((2 * tq, HEAD), F32),
                           pltpu.VMEM((2 * tq, HEAD), BF16)]),
        compiler_params=_cparams(("parallel", "parallel"), [((n_k, HEAD), BF16)] * 2 + [((tq, HEAD), BF16)] * 2,
                                 [((2 * tq, tk), F32)] * N_BUF + [((2 * tq, tk), BF16)] * N_BUF
                                 + [((2 * tq, HEAD), F32)] * (2 * N_BUF + 4)),
        name="diffattn",
    )(p, p, p, lam_vecs, subln_g)


def _outproj_kernel(u_ref, v_ref, gate_ref, of_ref, ob_ref, ws_ref, bs_ref, yc_ref, wa_ref, wb_ref, wc_ref,
                    x_ref, g1_ref, o_ref, ya_ref, yb_ref):
    tm = x_ref.shape[0]
    for n in range(tm // SGU_CHUNK):
        rs = slice(n * SGU_CHUNK, (n + 1) * SGU_CHUNK)
        for g in range(u_ref.shape[1] // HEAD):
            cs = slice(g * HEAD, (g + 1) * HEAD)
            mixed = jnp.dot(ws_ref[g], v_ref[rs, cs], preferred_element_type=F32) + bs_ref[g]
            ya_ref[rs, cs] = (u_ref[rs, cs].astype(F32) * mixed).astype(BF16)
    for h in range(of_ref.shape[1] // HEAD):
        cs = slice(h * HEAD, (h + 1) * HEAD)
        o = of_ref[:, cs] + ob_ref[:, cs]
        yb_ref[:, cs] = (_group_rms(o) * gate_ref[:, cs].astype(F32)).astype(BF16)
    acc = jnp.dot(ya_ref[...], wa_ref[...], preferred_element_type=F32)
    acc = acc + jnp.dot(yb_ref[...], wb_ref[...], preferred_element_type=F32)
    acc = acc + jnp.dot(yc_ref[...], wc_ref[...], preferred_element_type=F32)
    o_ref[...] = x_ref[...] + g1_ref[...] * acc


def _outproj(p, o_f, o_b, ws, bs, yc, y_row0, w, x, mods, li, row):
    m, d = x.shape
    dq = d // 4
    tm = _tile(m, 512)
    assert y_row0 % tm == 0 and tm % SGU_CHUNK == 0
    yb0 = y_row0 // tm
    tok = lambda col: pl.BlockSpec((tm, dq), lambda i: (yb0 + i, col))
    layer = lambda a: pl.BlockSpec((None,) + a.shape[1:], lambda i: (li,) + (0,) * (a.ndim - 1))
    return pl.pallas_call(
        _outproj_kernel,
        out_shape=jax.ShapeDtypeStruct((m, d), F32),
        grid=(m // tm,),
        in_specs=[
            tok(0), tok(1), tok(5), tok(0), tok(0), layer(ws), layer(bs),
            pl.BlockSpec((tm, 2 * dq), lambda i: (i, 0)),
            pl.BlockSpec((None, dq, d), lambda i: (li, 0, 0)),
            pl.BlockSpec((None, dq, d), lambda i: (li, 1, 0)),
            pl.BlockSpec((None, 2 * dq, d), lambda i: (li, 1, 0)),
            pl.BlockSpec((tm, d), lambda i: (i, 0)),
            pl.BlockSpec((None, None, None, 1, d), lambda i: (li, row, 2, 0, 0)),
        ],
        out_specs=pl.BlockSpec((tm, d), lambda i: (i, 0)),
        scratch_shapes=[pltpu.VMEM((tm, dq), BF16), pltpu.VMEM((tm, dq), BF16)],
        compiler_params=_cparams(("parallel",), [((tm, d), BF16), ((d, d), BF16), ((tm, d), F32), ((tm, d), F32),
                                                 ((tm, dq), F32), ((tm, dq), F32)], [((tm, d), BF16)]),
        name="outproj",
    )(p, p, p, o_f, o_b, ws, bs, yc, w, w, w, x, mods)


def _ffn_kernel(x_ref, sh_ref, sc_ref, gate_ref, g_ref, wg_ref, wu_ref, wd_ref, fg_ref, o_ref, h_ref,
                *, final_norm, n_ff_steps):
    f = pl.program_id(1)

    nf = pl.num_programs(1)

    def swiglu_slice(first, last):
        h = h_ref[...]
        a = jnp.dot(h, wg_ref[...].astype(BF16), preferred_element_type=F32)
        b = jnp.dot(h, wu_ref[...].astype(BF16), preferred_element_type=F32)
        act = (a * jax.nn.sigmoid(a) * b).astype(BF16)
        acc = jnp.dot(act, wd_ref[...].astype(BF16), preferred_element_type=F32)
        if not first:
            acc = o_ref[...] + acc
        if last:
            acc = x_ref[...] + gate_ref[...] * acc
            if final_norm:
                acc = acc * lax.rsqrt(jnp.mean(acc * acc, axis=-1, keepdims=True) + EPS) * fg_ref[...]
        o_ref[...] = acc

    @pl.when(f == 0)
    def _():
        h_ref[...] = _modulated_norm(x_ref[...], g_ref[...], sh_ref[...], sc_ref[...]).astype(BF16)
        swiglu_slice(True, n_ff_steps == 1)

    if n_ff_steps > 2:
        @pl.when((f != 0) & (f != nf - 1))
        def _():
            swiglu_slice(False, False)

    if n_ff_steps > 1:
        @pl.when(f == nf - 1)
        def _():
            swiglu_slice(False, True)


def _ffn(x, mods, li, row, g, wg, wu, wd, final_g, final_norm):
    m, d = x.shape
    ff = wg.shape[2]
    tm = _tile(m, 1024)
    tf = _tile(ff, 256)
    mod = lambda c: pl.BlockSpec((None, None, None, 1, d), lambda i, f: (li, row, c, 0, 0))
    return pl.pallas_call(
        functools.partial(_ffn_kernel, final_norm=final_norm, n_ff_steps=ff // tf),
        out_shape=jax.ShapeDtypeStruct((m, d), F32),
        grid=(m // tm, ff // tf),
        in_specs=[
            pl.BlockSpec((tm, d), lambda i, f: (i, 0), pipeline_mode=pl.Buffered(1)),
            mod(3), mod(4), mod(5),
            pl.BlockSpec((None, 1, d), lambda i, f: (li, 0, 0)),
            pl.BlockSpec((None, d, tf), lambda i, f: (li, 0, f)),
            pl.BlockSpec((None, d, tf), lambda i, f: (li, 0, f)),
            pl.BlockSpec((None, tf, d), lambda i, f: (li, f, 0)),
            pl.BlockSpec((1, d), lambda i, f: (0, 0)),
        ],
        out_specs=pl.BlockSpec((tm, d), lambda i, f: (i, 0)),
        scratch_shapes=[pltpu.VMEM((tm, d), BF16)],
        compiler_params=_cparams(("parallel", "arbitrary"),
                                 [((tm, d), F32), ((tm, d), F32)] + [((d, tf), wg.dtype)] * 3, [((tm, d), BF16)]),
        name="ffn",
    )(x, mods, mods, mods, g, wg, wu, wd, final_g)


def _rope_tables(n_rows, identity):
    def one(width):
        half = width // 2
        lane = np.arange(HEAD)
        sub = lane // width
        idx = lane % width
        first = idx < half
        freqs = np.float32(ROPE_BASE) ** (-(idx % half).astype(np.float32) / np.float32(half))
        t = np.arange(n_rows)
        pos = np.where((sub % 2) == 0, (t // GRID_W)[:, None], (t % GRID_W)[:, None]).astype(np.float32)
        ang = pos * freqs[None, :]
        cos, sin = np.cos(ang), np.sin(ang)
        if identity:
            cos, sin = np.ones_like(cos), np.zeros_like(sin)
        return [cos, np.where(first[None, :], -sin, 0.0), np.where(first[None, :], 0.0, sin)]
    return jnp.asarray(np.stack(one(HEAD // 2) + one(HEAD // 4)).astype(np.float32))


def kernel(x, c, ctx, c_ctx, w_ada, b_ada, norm1_g, w_in, sgu_w, sgu_b, ret_decay_fwd, ret_decay_bwd,
           diff_lambda_q1, diff_lambda_k1, diff_lambda_q2, diff_lambda_k2, diff_subln_g, w_out, norm2_g,
           w_gate, w_up, w_down, final_g):
    assert x.shape[0] == 1 and ctx.shape[0] == 1
    depth = w_ada.shape[0]
    xl, xc = x[0], ctx[0]
    n_lat, d = xl.shape
    n_ctx = xc.shape[0]
    total = n_lat + n_ctx
    dq = d // 4

    mods = _ada(jnp.concatenate([c, c_ctx[None, :]], axis=0).T, w_ada, b_ada)
    mods = mods.reshape(depth, 2, N_MOD, 1, d)
    tabs_lat = _rope_tables(n_lat, identity=False)
    tabs_ctx = _rope_tables(n_ctx, identity=True)
    w_out_b = w_out.astype(BF16)
    sgu_w_b = sgu_w.astype(BF16)
    sgu_b_col = sgu_b[..., None]
    g1, g2, fg = norm1_g[:, None, :], norm2_g[:, None, :], final_g[None, :]
    dec = jnp.stack([ret_decay_fwd, ret_decay_bwd]).astype(F32)
    lam_vecs = jnp.stack([diff_lambda_q1, diff_lambda_k1, diff_lambda_q2, diff_lambda_k2], axis=1).astype(F32)
    sub_g = diff_subln_g[:, None, :]

    for li in range(depth):
        need_ctx = li < depth - 1
        last = li == depth - 1
        lam_init = 0.8 - 0.6 * math.exp(-0.3 * li)
        p = _inproj(xl, mods, li, 0, g1, w_in, tabs_lat, total, 0)
        p = _inproj(xc, mods, li, 1, g1, w_in, tabs_ctx, total, n_lat, p_prev=p)
        o_f, o_b = _retention(p, dec, li, n_lat, n_ctx, dq)
        y_c = _attention(p, lam_vecs, sub_g, li, 0, n_lat, 0, total, d, lam_init)
        xl1 = _outproj(p, o_f, o_b, sgu_w_b, sgu_b_col, y_c, 0, w_out_b, xl, mods, li, 0)
        xl_new = _ffn(xl1, mods, li, 0, g2, w_gate, w_up, w_down, fg, last)
        if need_ctx:
            yc_c = _attention(p, lam_vecs, sub_g, li, n_lat, n_ctx, n_lat, n_ctx, d, lam_init)
            xc1 = _outproj(p, o_f, o_b, sgu_w_b, sgu_b_col, yc_c, n_lat, w_out_b, xc, mods, li, 1)
            xc = _ffn(xc1, mods, li, 1, g2, w_gate, w_up, w_down, fg, False)
        xl = xl_new
    return xl[None]
```
